```python
import math, functools
import jax, jax.numpy as jnp
from jax import lax
import numpy as np

D_MODEL = 1024
BATCH = 8
SEQ = 4096
DEPTH = 1
DEC_BATCH = 128
DEC_SEQ = 8
PAST_LEN = 16384
PAGE_SIZE = 128

MLA_HEADS = 8
MLA_NOPE = 64
MLA_ROPE = 32
MLA_V = 64
MLA_Q_LORA = D_MODEL // 4
MLA_KV_LORA = D_MODEL // 8
MLA_LAT = MLA_KV_LORA + MLA_ROPE
MLA_SCALE = (MLA_NOPE + MLA_ROPE) ** -0.5
DIFF_HEADS = 4
DIFF_KV_HEADS = 2
DIFF_GROUP = DIFF_HEADS // DIFF_KV_HEADS
DIFF_DH = 64
DIFF_SCALE = DIFF_DH ** -0.5
REL_BUCKETS = 32
REL_MAX_DIST = 128
D_FF = ((8 * D_MODEL // 3 + 127) // 128) * 128
ROPE_THETA = 10000.0
NORM_EPS = 1e-6
Q_BLOCK = 128
MASK_VALUE = -1e30
IN_WIDTHS = (MLA_Q_LORA, MLA_KV_LORA, MLA_ROPE, DIFF_HEADS * 2 * DIFF_DH,
             DIFF_KV_HEADS * 2 * DIFF_DH, DIFF_KV_HEADS * 2 * DIFF_DH, 2 * D_MODEL)
D_IN = (MLA_Q_LORA + MLA_KV_LORA + MLA_ROPE + DIFF_HEADS * 2 * DIFF_DH
        + 2 * DIFF_KV_HEADS * 2 * DIFF_DH + 2 * D_MODEL)

kernel_name = 'hybrid_mla_diffattn_macaron_step'


def rms_norm(x, g):
    xf = x.astype(jnp.float32)
    y = xf * lax.rsqrt(jnp.mean(xf * xf, axis=-1, keepdims=True) + NORM_EPS)
    return (y * g.astype(jnp.float32)).astype(x.dtype)


def lambda_init(layer):
    return 0.8 - 0.6 * math.exp(-0.3 * layer)


def apply_rope(x, pos):
    half = MLA_ROPE // 2
    inv_freq = ROPE_THETA ** (-jnp.arange(half, dtype=jnp.float32) / half)
    ang = pos.astype(jnp.float32)[:, None] * inv_freq[None, :]
    cos = jnp.cos(ang)[None, :, None, :]
    sin = jnp.sin(ang)[None, :, None, :]
    xf = x.astype(jnp.float32)
    x1, x2 = xf[..., :half], xf[..., half:]
    return jnp.concatenate([x1 * cos - x2 * sin, x1 * sin + x2 * cos], axis=-1).astype(x.dtype)


def rel_bucket(dist):
    n = jnp.maximum(dist, 0)
    max_exact = REL_BUCKETS // 2
    large = max_exact + (jnp.log(jnp.maximum(n, 1).astype(jnp.float32) / max_exact)
                         / math.log(REL_MAX_DIST / max_exact)
                         * (REL_BUCKETS - max_exact)).astype(jnp.int32)
    large = jnp.minimum(large, REL_BUCKETS - 1)
    return jnp.where(n < max_exact, n, large)


def rel_bias(table, q_pos, k_pos):
    bucket = rel_bucket(q_pos[:, None] - k_pos[None, :])
    bias = jnp.moveaxis(table.astype(jnp.float32)[bucket], -1, 0)
    return bias.reshape(DIFF_KV_HEADS, DIFF_GROUP, 1, q_pos.shape[0], k_pos.shape[0])


def ffn_half(x, pre_g, w_gu, w_down, post_g):
    h = rms_norm(x, pre_g)
    gate, up = jnp.split(h @ w_gu, 2, axis=-1)
    return x + 0.5 * rms_norm((jax.nn.silu(gate) * up) @ w_down, post_g)


def mixer_inputs(h, pos, w_in, q_norm_g, w_uq, kv_norm_g, w_uk):
    B, S, _ = h.shape
    splits = np.cumsum(IN_WIDTHS)[:-1].tolist()
    cq, ckv, kr, dq, dk, dv, gates = jnp.split(h @ w_in, splits, axis=-1)
    q = (rms_norm(cq, q_norm_g) @ w_uq).reshape(B, S, MLA_HEADS, MLA_NOPE + MLA_ROPE)
    q_lat = jnp.einsum('bshn,lhn->bshl', q[..., :MLA_NOPE], w_uk)
    q_mla = jnp.concatenate([q_lat, apply_rope(q[..., MLA_NOPE:], pos)], axis=-1)
    k_rope = apply_rope(kr[:, :, None, :], pos)[:, :, 0]
    lat = jnp.concatenate([rms_norm(ckv, kv_norm_g), k_rope], axis=-1)
    q_diff = dq.reshape(B, S, DIFF_KV_HEADS, DIFF_GROUP, 2, DIFF_DH)
    kv = jnp.stack([dk.reshape(B, S, DIFF_KV_HEADS, 2 * DIFF_DH),
                    dv.reshape(B, S, DIFF_KV_HEADS, 2 * DIFF_DH)], axis=2)
    return q_mla, lat, q_diff, kv, gates


def mla_scores(q_mla, lat):
    return jnp.einsum('bqhc,bkc->bhqk', q_mla, lat).astype(jnp.float32) * MLA_SCALE


def diff_scores(q_diff, kv, bias):
    b, k = kv.shape[:2]
    keys = kv[:, :, 0].reshape(b, k, DIFF_KV_HEADS, 2, DIFF_DH)
    s = jnp.einsum('bqhgmd,bkhmd->bhgmqk', q_diff, keys).astype(jnp.float32) * DIFF_SCALE
    return s + bias


def prompt_attention(q_mla, lat, q_diff, kv, lam, rel_table):
    B, S = lat.shape[:2]
    nb = S // Q_BLOCK
    k_pos = jnp.arange(S)
    v_lat = lat[..., :MLA_KV_LORA].astype(jnp.float32)
    v_diff = kv[:, :, 1].astype(jnp.float32)

    def block(args):
        qm, qd, i = args
        q_pos = i * Q_BLOCK + jnp.arange(Q_BLOCK)
        causal = k_pos[None, :] <= q_pos[:, None]
        p_m = jax.nn.softmax(jnp.where(causal, mla_scores(qm, lat), MASK_VALUE), axis=-1)
        o_m = jnp.einsum('bhqk,bkl->bqhl', p_m, v_lat)
        s_d = diff_scores(qd, kv, rel_bias(rel_table, q_pos, k_pos))
        p_d = jax.nn.softmax(jnp.where(causal, s_d, MASK_VALUE), axis=-1)
        a_d = p_d[:, :, :, 0] - lam * p_d[:, :, :, 1]
        o_d = jnp.einsum('bhgqk,bkhe->bqhge', a_d, v_diff)
        return o_m, o_d

    to_blocks = lambda t: jnp.moveaxis(t.reshape(B, nb, Q_BLOCK, *t.shape[2:]), 1, 0)
    o_m, o_d = lax.map(block, (to_blocks(q_mla), to_blocks(q_diff), jnp.arange(nb)))
    from_blocks = lambda t: jnp.moveaxis(t, 0, 1).reshape(B, S, *t.shape[3:])
    return from_blocks(o_m), from_blocks(o_d)


def online_update(state, s, v, eq):
    m, l, acc = state
    m_new = jnp.maximum(m, s.max(axis=-1))
    alpha = jnp.exp(m - m_new)
    p = jnp.exp(s - m_new[..., None])
    return (m_new, l * alpha + p.sum(axis=-1),
            acc * alpha[..., None] + jnp.einsum(eq, p, v.astype(jnp.float32)))


def absorb_keys(carry, q_mla, q_diff, lat_k, kv_k, q_pos, k_pos, rel_table, mask):
    mla_st, diff_st = carry
    s_m = mla_scores(q_mla, lat_k)
    s_d = diff_scores(q_diff, kv_k, rel_bias(rel_table, q_pos, k_pos))
    if mask is not None:
        s_m = jnp.where(mask, s_m, MASK_VALUE)
        s_d = jnp.where(mask, s_d, MASK_VALUE)
    mla_st = online_update(mla_st, s_m, lat_k[..., :MLA_KV_LORA], 'bhqk,bkl->bhql')
    diff_st = online_update(diff_st, s_d, kv_k[:, :, 1], 'bhgmqk,bkhe->bhgmqe')
    return mla_st, diff_st


def sample_attention(q_mla, lat, q_diff, kv, lam, cache_mla, cache_diff, page_table, layer, rel_table):
    Bd, T = lat.shape[:2]
    n_pages = page_table.shape[1]
    q_pos = PAST_LEN + jnp.arange(T)
    f32 = jnp.float32

    def init(shape, width):
        return (jnp.full(shape, -jnp.inf, f32), jnp.zeros(shape, f32), jnp.zeros(shape + (width,), f32))

    carry = (init((Bd, MLA_HEADS, T), MLA_KV_LORA),
             init((Bd, DIFF_KV_HEADS, DIFF_GROUP, 2, T), 2 * DIFF_DH))

    def step(carry, xs):
        phys, j = xs
        k_pos = j * PAGE_SIZE + jnp.arange(PAGE_SIZE)
        lat_k = cache_mla[layer, phys]
        kv_k = cache_diff[layer, phys]
        return absorb_keys(carry, q_mla, q_diff, lat_k, kv_k, q_pos, k_pos, rel_table, None), None

    carry, _ = lax.scan(step, carry, (page_table.T, jnp.arange(n_pages)))
    causal = q_pos[None, :] <= q_pos[:, None]
    (_, l_m, acc_m), (_, l_d, acc_d) = absorb_keys(carry, q_mla, q_diff, lat, kv, q_pos, q_pos, rel_table, causal)
    o_m = jnp.moveaxis(acc_m / l_m[..., None], 2, 1)
    o = acc_d / l_d[..., None]
    o_d = jnp.moveaxis(o[:, :, :, 0] - lam * o[:, :, :, 1], 3, 1)
    return o_m, o_d


def diff_lambda(lq1, lk1, lq2, lk2, lam_init):
    f = jnp.float32
    return (jnp.exp(jnp.sum(lq1.astype(f) * lk1.astype(f)))
            - jnp.exp(jnp.sum(lq2.astype(f) * lk2.astype(f))) + lam_init)


def mixer_merge(o_m, o_d, gates, w_uv, subln_g, lam_init, w_o_mla, w_o_diff, w_out, dtype):
    B, S = o_m.shape[:2]
    v_a = jnp.einsum('bshl,lhv->bshv', o_m.astype(dtype), w_uv).reshape(B, S, MLA_HEADS * MLA_V)
    y_a = v_a @ w_o_mla
    o_d = (rms_norm(o_d, subln_g) * (1.0 - lam_init)).astype(dtype)
    y_b = o_d.reshape(B, S, DIFF_HEADS * 2 * DIFF_DH) @ w_o_diff
    g_a, g_b = jnp.split(gates, 2, axis=-1)
    return (jax.nn.sigmoid(g_a) * y_a + jax.nn.sigmoid(g_b) * y_b) @ w_out


def decoder_layer(x, pos, attend, lam, lam_init, p):
    x = ffn_half(x, p['ffn1_pre_g'], p['ffn1_w_gu'], p['ffn1_w_down'], p['ffn1_post_g'])
    h = rms_norm(x, p['mix_pre_g'])
    q_mla, lat, q_diff, kv, gates = mixer_inputs(h, pos, p['w_in'], p['mla_q_norm_g'], p['w_uq'],
                                                 p['mla_kv_norm_g'], p['w_uk'])
    o_m, o_d = attend(q_mla, lat, q_diff, kv, lam)
    y = mixer_merge(o_m, o_d, gates, p['w_uv'], p['diff_subln_g'], lam_init,
                    p['w_o_mla'], p['w_o_diff'], p['w_out'], x.dtype)
    x = x + rms_norm(y, p['mix_post_g'])
    x = ffn_half(x, p['ffn2_pre_g'], p['ffn2_w_gu'], p['ffn2_w_down'], p['ffn2_post_g'])
    return x, lat, kv


def setup_inputs(seed: int = 0) -> dict:
    key = jax.random.key(seed)
    ks = iter(jax.random.split(key, 48))
    f32 = jnp.float32
    n_pages = PAST_LEN // PAGE_SIZE
    n_pool = (5 * DEC_BATCH * n_pages) // 4

    def w(shape, fan_in):
        return jax.random.normal(next(ks), (DEPTH,) + shape, f32) * fan_in ** -0.5

    def gain(n):
        return 1.0 + 0.01 * jax.random.normal(next(ks), (DEPTH, n), f32)

    def small(shape, scale):
        return scale * jax.random.normal(next(ks), shape, f32)

    page_table = jax.random.permutation(next(ks), n_pool)[: DEC_BATCH * n_pages]
    page_table = page_table.reshape(DEC_BATCH, n_pages).astype(jnp.int32)
    return {
        'x_prompt': jax.random.normal(next(ks), (BATCH, SEQ, D_MODEL), f32),
        'x_sample': jax.random.normal(next(ks), (DEC_BATCH, DEC_SEQ, D_MODEL), f32),
        'cache_mla': jax.random.normal(next(ks), (DEPTH, n_pool, PAGE_SIZE, MLA_LAT), f32),
        'cache_diff': jax.random.normal(next(ks), (DEPTH, n_pool, PAGE_SIZE, 2, DIFF_KV_HEADS, 2 * DIFF_DH), f32),
        'page_table': page_table,
        'ffn1_pre_g': gain(D_MODEL),
        'ffn1_w_gu': w((D_MODEL, 2 * D_FF), D_MODEL),
        'ffn1_w_down': w((D_FF, D_MODEL), D_FF),
        'ffn1_post_g': gain(D_MODEL),
        'mix_pre_g': gain(D_MODEL),
        'w_in': w((D_MODEL, D_IN), D_MODEL),
        'mla_q_norm_g': gain(MLA_Q_LORA),
        'w_uq': w((MLA_Q_LORA, MLA_HEADS * (MLA_NOPE + MLA_ROPE)), MLA_Q_LORA),
        'mla_kv_norm_g': gain(MLA_KV_LORA),
        'w_uk': w((MLA_KV_LORA, MLA_HEADS, MLA_NOPE), MLA_KV_LORA),
        'w_uv': w((MLA_KV_LORA, MLA_HEADS, MLA_V), MLA_KV_LORA),
        'diff_lq1': small((DEPTH, DIFF_DH), 0.1),
        'diff_lk1': small((DEPTH, DIFF_DH), 0.1),
        'diff_lq2': small((DEPTH, DIFF_DH), 0.1),
        'diff_lk2': small((DEPTH, DIFF_DH), 0.1),
        'diff_subln_g': gain(2 * DIFF_DH),
        'rel_table': small((REL_BUCKETS, DIFF_HEADS), 0.2),
        'w_o_mla': w((MLA_HEADS * MLA_V, D_MODEL), MLA_HEADS * MLA_V),
        'w_o_diff': w((DIFF_HEADS * 2 * DIFF_DH, D_MODEL), DIFF_HEADS * 2 * DIFF_DH),
        'w_out': w((D_MODEL, D_MODEL), D_MODEL),
        'mix_post_g': gain(D_MODEL),
        'ffn2_pre_g': gain(D_MODEL),
        'ffn2_w_gu': w((D_MODEL, 2 * D_FF), D_MODEL),
        'ffn2_w_down': w((D_FF, D_MODEL), D_FF),
        'ffn2_post_g': gain(D_MODEL),
    }


def reference(x_prompt, x_sample, cache_mla, cache_diff, page_table,
              ffn1_pre_g, ffn1_w_gu, ffn1_w_down, ffn1_post_g,
              mix_pre_g, w_in, mla_q_norm_g, w_uq, mla_kv_norm_g, w_uk, w_uv,
              diff_lq1, diff_lk1, diff_lq2, diff_lk2, diff_subln_g, rel_table,
              w_o_mla, w_o_diff, w_out, mix_post_g,
              ffn2_pre_g, ffn2_w_gu, ffn2_w_down, ffn2_post_g):
    pos_p = jnp.arange(x_prompt.shape[1])
    pos_s = PAST_LEN + jnp.arange(x_sample.shape[1])
    y_p, y_s = x_prompt, x_sample
    mla_p, diff_p, mla_s, diff_s = [], [], [], []
    for layer in range(DEPTH):
        p = {
            'ffn1_pre_g': ffn1_pre_g[layer], 'ffn1_w_gu': ffn1_w_gu[layer],
            'ffn1_w_down': ffn1_w_down[layer], 'ffn1_post_g': ffn1_post_g[layer],
            'mix_pre_g': mix_pre_g[layer], 'w_in': w_in[layer],
            'mla_q_norm_g': mla_q_norm_g[layer], 'w_uq': w_uq[layer],
            'mla_kv_norm_g': mla_kv_norm_g[layer], 'w_uk': w_uk[layer], 'w_uv': w_uv[layer],
            'diff_subln_g': diff_subln_g[layer], 'w_o_mla': w_o_mla[layer],
            'w_o_diff': w_o_diff[layer], 'w_out': w_out[layer], 'mix_post_g': mix_post_g[layer],
            'ffn2_pre_g': ffn2_pre_g[layer], 'ffn2_w_gu': ffn2_w_gu[layer],
            'ffn2_w_down': ffn2_w_down[layer], 'ffn2_post_g': ffn2_post_g[layer],
        }
        lam_init = lambda_init(layer)
        lam = diff_lambda(diff_lq1[layer], diff_lk1[layer], diff_lq2[layer], diff_lk2[layer], lam_init)
        prompt_attend = functools.partial(prompt_attention, rel_table=rel_table)
        sample_attend = functools.partial(sample_attention, cache_mla=cache_mla, cache_diff=cache_diff,
                                          page_table=page_table, layer=layer, rel_table=rel_table)
        y_p, lat_p, kv_p = decoder_layer(y_p, pos_p, prompt_attend, lam, lam_init, p)
        y_s, lat_s, kv_s = decoder_layer(y_s, pos_s, sample_attend, lam, lam_init, p)
        mla_p.append(lat_p)
        diff_p.append(kv_p)
        mla_s.append(lat_s)
        diff_s.append(kv_s)
    return (y_p, y_s, jnp.stack(mla_p), jnp.stack(diff_p), jnp.stack(mla_s), jnp.stack(diff_s))
```

```python
import functools
import math

import numpy as np
import jax
import jax.numpy as jnp
from jax import lax
from jax.experimental import pallas as pl
from jax.experimental.pallas import tpu as pltpu

F32 = jnp.float32
BF16 = jnp.bfloat16

D_MODEL = 1024
MLA_HEADS = 8
MLA_NOPE = 64
MLA_ROPE = 32
MLA_V = 64
MLA_Q_LORA = D_MODEL // 4
MLA_KV_LORA = D_MODEL // 8
MLA_LAT = MLA_KV_LORA + MLA_ROPE
MLA_SCALE = (MLA_NOPE + MLA_ROPE) ** -0.5
DIFF_HEADS = 4
DIFF_KV_HEADS = 2
DIFF_GROUP = DIFF_HEADS // DIFF_KV_HEADS
DIFF_DH = 64
DIFF_SCALE = DIFF_DH ** -0.5
REL_BUCKETS = 32
REL_MAX_DIST = 128
D_FF = ((8 * D_MODEL // 3 + 127) // 128) * 128
ROPE_THETA = 10000.0
NORM_EPS = 1e-6
MASK_VALUE = -1e30
PAGE_SIZE = 128

LANES = 128
QPAD = 2 * LANES
VMEM_LIMIT = 56 * 1024 * 1024

FFN_TF = 256
PROMPT_TQ = 256
PROMPT_TK = 256
DEC_PAGES = 16
DEC_TK = DEC_PAGES * PAGE_SIZE


def _bucket_thresholds():
    max_exact = REL_BUCKETS // 2
    n = np.arange(0, 4 * REL_MAX_DIST)
    large = max_exact + (np.log(np.maximum(n, 1).astype(np.float32) / max_exact)
                         / math.log(REL_MAX_DIST / max_exact)
                         * (REL_BUCKETS - max_exact)).astype(np.int32)
    bucket = np.where(n < max_exact, n, np.minimum(large, REL_BUCKETS - 1))
    assert np.all(np.diff(bucket) >= 0)
    return [int(np.argmax(bucket >= k)) for k in range(REL_BUCKETS)]


_BUCKET_THR = _bucket_thresholds()
REL_FAR = _BUCKET_THR[REL_BUCKETS - 1]


def _rms(x, g):
    return x * lax.rsqrt(jnp.mean(x * x, axis=-1, keepdims=True) + NORM_EPS) * g


def _dot(a, b):
    return jnp.dot(a, b, preferred_element_type=F32)


def _dot_nt(a, b):
    return lax.dot_general(a, b, (((1,), (1,)), ((), ())), preferred_element_type=F32)


def _cparams(*sem):
    return pltpu.CompilerParams(dimension_semantics=sem, vmem_limit_bytes=VMEM_LIMIT)


def _ffn_kernel(x_ref, pre_g_ref, wg_ref, wu_ref, wd_ref, post_g_ref, o_ref, h_ref, acc_ref):
    f = pl.program_id(1)

    @pl.when(f == 0)
    def _():
        h_ref[...] = _rms(x_ref[...], pre_g_ref[...]).astype(BF16)
        acc_ref[...] = jnp.zeros_like(acc_ref)

    h = h_ref[...]
    gate = _dot(h, wg_ref[...])
    up = _dot(h, wu_ref[...])
    act = (gate * jax.nn.sigmoid(gate) * up).astype(BF16)
    acc_ref[...] += _dot(act, wd_ref[...])

    @pl.when(f == pl.num_programs(1) - 1)
    def _():
        o_ref[...] = x_ref[...] + 0.5 * _rms(acc_ref[...], post_g_ref[...])


def _ffn(x, pre_g, w_gu, w_down, post_g, tm):
    n = x.shape[0]
    nf = D_FF // FFN_TF
    return pl.pallas_call(
        _ffn_kernel,
        out_shape=jax.ShapeDtypeStruct((n, D_MODEL), F32),
        grid=(n // tm, nf),
        in_specs=[
            pl.BlockSpec((tm, D_MODEL), lambda i, f: (i, 0)),
            pl.BlockSpec((1, D_MODEL), lambda i, f: (0, 0)),
            pl.BlockSpec((D_MODEL, FFN_TF), lambda i, f: (0, f)),
            pl.BlockSpec((D_MODEL, FFN_TF), lambda i, f: (0, f + D_FF // FFN_TF)),
            pl.BlockSpec((FFN_TF, D_MODEL), lambda i, f: (f, 0)),
            pl.BlockSpec((1, D_MODEL), lambda i, f: (0, 0)),
        ],
        out_specs=pl.BlockSpec((tm, D_MODEL), lambda i, f: (i, 0)),
        scratch_shapes=[pltpu.VMEM((tm, D_MODEL), BF16), pltpu.VMEM((tm, D_MODEL), F32)],
        compiler_params=_cparams("parallel", "arbitrary"),
        name="ffn_half",
    )(x, pre_g, w_gu, w_gu, w_down, post_g)


def _absorb_kernel(wq_ref, wk_ref, o_ref):
    for h in range(MLA_HEADS):
        o_ref[:, h * LANES:(h + 1) * LANES] = _dot(wq_ref[h], wk_ref[h]).astype(BF16)


def _absorb(wq_nope, wk_t):
    return pl.pallas_call(
        _absorb_kernel,
        out_shape=jax.ShapeDtypeStruct((MLA_Q_LORA, MLA_HEADS * MLA_KV_LORA), BF16),
        name="absorb_uk",
    )(wq_nope, wk_t)


def _bias_from_dist(dist, table_ref, head):
    val = jnp.full(dist.shape, table_ref[0, head], F32)
    for k in range(1, REL_BUCKETS):
        val = jnp.where(dist >= _BUCKET_THR[k], table_ref[k, head], val)
    return jnp.where(dist < 0, MASK_VALUE, val)


def _bias_kernel(table_ref, bp_ref, bd_ref, bn_ref, *, past_len, dec_t):
    r = lax.broadcasted_iota(jnp.int32, (PROMPT_TQ, PROMPT_TK), 0)
    c = lax.broadcasted_iota(jnp.int32, (PROMPT_TQ, PROMPT_TK), 1)
    for head in range(DIFF_HEADS):
        for kind in range(2):
            bp_ref[head, kind] = _bias_from_dist(r - c + kind * PROMPT_TK, table_ref, head)
    t = lax.broadcasted_iota(jnp.int32, (dec_t, DEC_TK), 0)
    c = lax.broadcasted_iota(jnp.int32, (dec_t, DEC_TK), 1)
    tn = lax.broadcasted_iota(jnp.int32, (dec_t, LANES), 0)
    cn = lax.broadcasted_iota(jnp.int32, (dec_t, LANES), 1)
    for kvh in range(DIFF_KV_HEADS):
        for g in range(DIFF_GROUP):
            head = kvh * DIFF_GROUP + g
            far = _bias_from_dist(t - c + past_len, table_ref, head)
            last = _bias_from_dist(t - c + DEC_TK, table_ref, head)
            new = jnp.where(cn < dec_t, _bias_from_dist(tn - cn, table_ref, head), MASK_VALUE)
            for m in range(2):
                row = (g * 2 + m) * dec_t
                bd_ref[0, kvh, row:row + dec_t, :] = far
                bd_ref[1, kvh, row:row + dec_t, :] = last
                bn_ref[kvh, row:row + dec_t, :] = new


def _rel_bias_tiles(rel_table, past_len, dec_t):
    assert past_len - DEC_TK >= REL_FAR and dec_t <= LANES
    return pl.pallas_call(
        functools.partial(_bias_kernel, past_len=past_len, dec_t=dec_t),
        out_shape=(
            jax.ShapeDtypeStruct((DIFF_HEADS, 2, PROMPT_TQ, PROMPT_TK), F32),
            jax.ShapeDtypeStruct((2, DIFF_KV_HEADS, 4 * dec_t, DEC_TK), F32),
            jax.ShapeDtypeStruct((DIFF_KV_HEADS, 4 * dec_t, LANES), F32),
        ),
        in_specs=[pl.BlockSpec(memory_space=pltpu.SMEM)],
        name="rel_bias_tiles",
    )(rel_table)


def _mix_in_kernel(x_ref, pre_g_ref, w_in_ref, qg_ref, w_abs_ref, w_ra_ref, w_rb_ref, kvg_ref,
                   cos_ref, sin_ref,
                   q_ref, lat_ref, rows_ref, qd_ref, kd_ref, vd_ref, kv_ref):
    h = _rms(x_ref[...], pre_g_ref[...]).astype(BF16)
    p = _dot(h, w_in_ref[...])
    cq = p[:, 0:256]
    ckv = p[:, 256:384]
    kr_a = p[:, 384:512]
    kr_b = p[:, 512:640]
    dq = p[:, 640:1152]
    dk = p[:, 1152:1408]
    dv = p[:, 1408:1664]
    cos = cos_ref[...]
    sin = sin_ref[...]

    cqn = _rms(cq, qg_ref[...]).astype(BF16)
    q_lat = _dot(cqn, w_abs_ref[...])
    r_a = _dot(cqn, w_ra_ref[...])
    r_b = _dot(cqn, w_rb_ref[...])
    for hd in range(MLA_HEADS):
        sl = slice(hd * LANES, (hd + 1) * LANES)
        q_ref[hd, :, 0:LANES] = (q_lat[:, sl] * MLA_SCALE).astype(q_ref.dtype)
        q_ref[hd, :, LANES:QPAD] = ((r_a[:, sl] * cos + r_b[:, sl] * sin) * MLA_SCALE).astype(q_ref.dtype)

    ckvn = _rms(ckv, kvg_ref[...])
    k_rope = kr_a * cos + kr_b * sin
    lat_ref[:, 0:LANES] = ckvn.astype(BF16)
    lat_ref[:, LANES:QPAD] = k_rope.astype(BF16)
    rows_ref[:, 0:MLA_KV_LORA] = ckvn
    rows_ref[:, MLA_KV_LORA:MLA_LAT] = k_rope[:, 0:MLA_ROPE]

    lane = lax.broadcasted_iota(jnp.int32, (x_ref.shape[0], LANES), 1)
    for kvh in range(DIFF_KV_HEADS):
        for g in range(DIFF_GROUP):
            pair = dq[:, (kvh * 2 + g) * LANES:(kvh * 2 + g + 1) * LANES] * DIFF_SCALE
            for m in range(2):
                keep = (lane < DIFF_DH) if m == 0 else (lane >= DIFF_DH)
                qd_ref[kvh * 4 + g * 2 + m] = jnp.where(keep, pair, 0.0).astype(qd_ref.dtype)
    kd_ref[...] = dk.astype(BF16)
    vd_ref[...] = dv.astype(BF16)
    kv_ref[:, 0:256] = dk
    kv_ref[:, 256:512] = dv


def _mix_in(x, wl, cos_t, sin_t, tm, q_dtype):
    n = x.shape[0]
    n_tab = cos_t.shape[0] // tm
    const = lambda i: (0, 0)
    w = lambda a: pl.BlockSpec(a.shape, const)
    return pl.pallas_call(
        _mix_in_kernel,
        out_shape=(
            jax.ShapeDtypeStruct((MLA_HEADS, n, QPAD), q_dtype),
            jax.ShapeDtypeStruct((n, QPAD), BF16),
            jax.ShapeDtypeStruct((n, MLA_LAT), F32),
            jax.ShapeDtypeStruct((8, n, LANES), q_dtype),
            jax.ShapeDtypeStruct((n, 256), BF16),
            jax.ShapeDtypeStruct((n, 256), BF16),
            jax.ShapeDtypeStruct((n, 512), F32),
        ),
        grid=(n // tm,),
        in_specs=[
            pl.BlockSpec((tm, D_MODEL), lambda i: (i, 0)),
            w(wl['mix_pre_g']), w(wl['w_in_a']), w(wl['q_norm_g']), w(wl['w_abs']),
            w(wl['w_rope_a']), w(wl['w_rope_b']), w(wl['kv_norm_g']),
            pl.BlockSpec((tm, LANES), lambda i: (i % n_tab, 0)),
            pl.BlockSpec((tm, LANES), lambda i: (i % n_tab, 0)),
        ],
        out_specs=(
            pl.BlockSpec((MLA_HEADS, tm, QPAD), lambda i: (0, i, 0)),
            pl.BlockSpec((tm, QPAD), lambda i: (i, 0)),
            pl.BlockSpec((tm, MLA_LAT), lambda i: (i, 0)),
            pl.BlockSpec((8, tm, LANES), lambda i: (0, i, 0)),
            pl.BlockSpec((tm, 256), lambda i: (i, 0)),
            pl.BlockSpec((tm, 256), lambda i: (i, 0)),
            pl.BlockSpec((tm, 512), lambda i: (i, 0)),
        ),
        compiler_params=_cparams("parallel"),
        name="mix_in",
    )(x, wl['mix_pre_g'], wl['w_in_a'], wl['q_norm_g'], wl['w_abs'], wl['w_rope_a'], wl['w_rope_b'],
      wl['kv_norm_g'], cos_t, sin_t)


def _online_step(s, v, m, l, acc, shift=None, v_transposed=False):
    m_blk = jnp.max(s, axis=-1, keepdims=True)
    if shift is not None:
        m_blk = m_blk + shift
    m_new = jnp.maximum(m, m_blk)
    alpha = jnp.exp(m - m_new)
    p = jnp.exp(s - (m_new if shift is None else m_new - shift))
    l_new = alpha * l + jnp.sum(p, axis=-1, keepdims=True)
    pv = _dot_nt(p.astype(BF16), v) if v_transposed else _dot(p.astype(BF16), v)
    acc_new = alpha * acc + pv
    return m_new, l_new, acc_new


def _mla_prompt_kernel(q_ref, lat_ref, o_ref, m_ref, l_ref, acc_ref):
    i = pl.program_id(1)
    tq, tk = PROMPT_TQ, PROMPT_TK
    rows = MLA_HEADS * tq
    q = q_ref[...].reshape(rows, QPAD)
    m_ref[...] = jnp.full_like(m_ref, -jnp.inf)
    l_ref[...] = jnp.zeros_like(l_ref)
    acc_ref[...] = jnp.zeros_like(acc_ref)

    def chunk(j, mask):
        k = lat_ref[pl.ds(pl.multiple_of(j * tk, tk), tk), :]
        s = _dot_nt(q, k)
        if mask is not None:
            s = jnp.where(mask, s, MASK_VALUE)
        m, l, acc = _online_step(s, k[:, 0:MLA_KV_LORA], m_ref[...], l_ref[...], acc_ref[...])
        m_ref[...] = m
        l_ref[...] = l
        acc_ref[...] = acc

    def body(j, carry):
        chunk(j, None)
        return carry

    lax.fori_loop(0, i, body, 0)
    r = lax.broadcasted_iota(jnp.int32, (MLA_HEADS, tq, tk), 1).reshape(rows, tk)
    c = lax.broadcasted_iota(jnp.int32, (rows, tk), 1)
    chunk(i, c <= r)

    inv = 1.0 / l_ref[...]
    for hd in range(MLA_HEADS):
        o_ref[:, hd * LANES:(hd + 1) * LANES] = (
            acc_ref[hd * tq:(hd + 1) * tq, :] * inv[hd * tq:(hd + 1) * tq, :]).astype(o_ref.dtype)


def _mla_prompt(q, lat, batch, seq):
    tq = PROMPT_TQ
    nq = seq // tq
    rows = MLA_HEADS * tq
    return pl.pallas_call(
        _mla_prompt_kernel,
        out_shape=jax.ShapeDtypeStruct((batch * seq, MLA_HEADS * MLA_KV_LORA), BF16),
        grid=(batch, nq),
        in_specs=[
            pl.BlockSpec((MLA_HEADS, tq, QPAD), lambda b, i: (0, b * nq + i, 0)),
            pl.BlockSpec((seq, QPAD), lambda b, i: (b, 0)),
        ],
        out_specs=pl.BlockSpec((tq, MLA_HEADS * MLA_KV_LORA), lambda b, i: (b * nq + i, 0)),
        scratch_shapes=[pltpu.VMEM((rows, 1), F32), pltpu.VMEM((rows, 1), F32),
                        pltpu.VMEM((rows, MLA_KV_LORA), F32)],
        compiler_params=_cparams("parallel", "arbitrary"),
        name="mla_prompt",
    )(q, lat)


def _diff_lambda(lq1, lk1, lq2, lk2, lam_init):
    return (jnp.exp(jnp.sum(lq1 * lk1, axis=-1, keepdims=True))
            - jnp.exp(jnp.sum(lq2 * lk2, axis=-1, keepdims=True)) + lam_init)


def _diff_combine(o0, o1, lam, subln_g, lam_init):
    return _rms(o0 - lam * o1, subln_g) * (1.0 - lam_init)


def _diff_prompt_kernel(q_ref, k_ref, v_ref, bias_ref, far_ref, lq1_ref, lk1_ref, lq2_ref, lk2_ref,
                        subln_ref, o_ref, m_ref, l_ref, acc_ref, *, lam_init):
    i = pl.program_id(2)
    tq, tk = PROMPT_TQ, PROMPT_TK
    rows = 4 * tq
    q = q_ref[...].reshape(rows, LANES)
    far = far_ref[0]
    m_ref[...] = jnp.full_like(m_ref, -jnp.inf)
    l_ref[...] = jnp.zeros_like(l_ref)
    acc_ref[...] = jnp.zeros_like(acc_ref)

    def chunk(j, kind):
        start = pl.multiple_of(j * tk, tk)
        s = _dot_nt(q, k_ref[pl.ds(start, tk), :])
        shift = far
        if kind is not None:
            s = (s.reshape(DIFF_GROUP, 2, tq, tk) + bias_ref[:, kind][:, None]).reshape(rows, tk)
            shift = None
        m, l, acc = _online_step(s, v_ref[pl.ds(start, tk), :], m_ref[...], l_ref[...], acc_ref[...], shift)
        m_ref[...] = m
        l_ref[...] = l
        acc_ref[...] = acc

    def body(j, carry):
        chunk(j, None)
        return carry

    lax.fori_loop(0, jnp.maximum(i - 1, 0), body, 0)

    @pl.when(i > 0)
    def _():
        chunk(i - 1, 1)

    chunk(i, 0)

    lam = _diff_lambda(lq1_ref[...], lk1_ref[...], lq2_ref[...], lk2_ref[...], lam_init)
    o = acc_ref[...] * (1.0 / l_ref[...])
    for g in range(DIFF_GROUP):
        o0 = o[(2 * g) * tq:(2 * g + 1) * tq, :]
        o1 = o[(2 * g + 1) * tq:(2 * g + 2) * tq, :]
        o_ref[:, g * LANES:(g + 1) * LANES] = _diff_combine(
            o0, o1, lam, subln_ref[...], lam_init).astype(o_ref.dtype)


def _diff_prompt(qd, kd, vd, bias_p, far_rows, wl, lam_init, batch, seq):
    tq = PROMPT_TQ
    nq = seq // tq
    rows = 4 * tq
    assert PROMPT_TQ == PROMPT_TK and PROMPT_TK >= REL_FAR
    const3 = lambda b, h, i: (0, 0)
    small = lambda a: pl.BlockSpec(a.shape, const3)
    return pl.pallas_call(
        functools.partial(_diff_prompt_kernel, lam_init=lam_init),
        out_shape=jax.ShapeDtypeStruct((batch * seq, DIFF_HEADS * 2 * DIFF_DH), BF16),
        grid=(batch, DIFF_KV_HEADS, nq),
        in_specs=[
            pl.BlockSpec((4, tq, LANES), lambda b, h, i: (h, b * nq + i, 0)),
            pl.BlockSpec((seq, LANES), lambda b, h, i: (b, h)),
            pl.BlockSpec((seq, LANES), lambda b, h, i: (b, h)),
            pl.BlockSpec((DIFF_GROUP, 2, tq, PROMPT_TK), lambda b, h, i: (h, 0, 0, 0)),
            pl.BlockSpec((1, rows, 1), lambda b, h, i: (h, 0, 0)),
            small(wl['lq1']), small(wl['lk1']), small(wl['lq2']), small(wl['lk2']), small(wl['subln_g']),
        ],
        out_specs=pl.BlockSpec((tq, DIFF_GROUP * LANES), lambda b, h, i: (b * nq + i, h)),
        scratch_shapes=[pltpu.VMEM((rows, 1), F32), pltpu.VMEM((rows, 1), F32),
                        pltpu.VMEM((rows, LANES), F32)],
        compiler_params=_cparams("parallel", "parallel", "arbitrary"),
        name="diff_prompt",
    )(qd, kd, vd, bias_p, far_rows, wl['lq1'], wl['lk1'], wl['lq2'], wl['lk2'], wl['subln_g'])


def _decode_kernel(pt_ref, qm_ref, qd_ref, rows_ref, kv_ref, bias_ref, biasn_ref,
                   lq1_ref, lk1_ref, lq2_ref, lk2_ref, subln_ref, cm_hbm, cd_hbm,
                   om_ref, od_ref, mbuf, dbuf, nlat, nkv, sem,
                   *, lam_init, layer, n_chunks, dec_t):
    b = pl.program_id(0)
    nb = pl.num_programs(0)
    mrows = MLA_HEADS * dec_t
    drows = 4 * dec_t
    prow = PAGE_SIZE * 4

    def copies(bb, c, slot):
        out = []
        for p in range(DEC_PAGES):
            phys = pt_ref[bb, c * DEC_PAGES + p]
            out.append(pltpu.make_async_copy(cm_hbm.at[layer, phys],
                                             mbuf.at[slot, :, pl.ds(p * PAGE_SIZE, PAGE_SIZE)], sem.at[0, slot]))
            out.append(pltpu.make_async_copy(cd_hbm.at[layer, phys], dbuf.at[slot, pl.ds(p * prow, prow)],
                                             sem.at[1, slot]))
        return out

    @pl.when(b == 0)
    def _():
        nlat[...] = jnp.zeros_like(nlat)
        nkv[...] = jnp.zeros_like(nkv)
        for cp in copies(0, 0, 0):
            cp.start()

    qm = qm_ref[...].reshape(mrows, QPAD).astype(BF16)
    qd = [qd_ref[kvh * 4:(kvh + 1) * 4].reshape(drows, LANES).astype(BF16) for kvh in range(DIFF_KV_HEADS)]

    def init(r, d):
        return (jnp.full((r, 1), -jnp.inf, F32), jnp.zeros((r, 1), F32), jnp.zeros((r, d), F32))

    def body(c, carry):
        st_m, st_d = carry
        slot = c % 2

        @pl.when(c + 1 < n_chunks)
        def _():
            for cp in copies(b, c + 1, 1 - slot):
                cp.start()

        @pl.when(jnp.logical_and(c + 1 == n_chunks, b + 1 < nb))
        def _():
            for cp in copies(b + 1, 0, 1 - slot):
                cp.start()

        for cp in copies(b, c, slot):
            cp.wait()

        kt = mbuf[slot].astype(BF16)
        st_m = _online_step(_dot(qm[:, 0:MLA_LAT], kt), kt[0:MLA_KV_LORA, :], *st_m, v_transposed=True)

        kind = lax.convert_element_type(c == n_chunks - 1, jnp.int32)
        new_d = []
        for kvh in range(DIFF_KV_HEADS):
            kd = dbuf[slot, pl.ds(kvh, DEC_TK, stride=4), :].astype(BF16)
            vd = dbuf[slot, pl.ds(2 + kvh, DEC_TK, stride=4), :].astype(BF16)
            s = _dot_nt(qd[kvh], kd) + bias_ref[kind, kvh]
            new_d.append(_online_step(s, vd, *st_d[kvh]))
        return st_m, tuple(new_d)

    st_m, st_d = lax.fori_loop(
        0, n_chunks, body,
        (init(mrows, MLA_KV_LORA), tuple(init(drows, LANES) for _ in range(DIFF_KV_HEADS))))

    nlat[0:dec_t, 0:MLA_KV_LORA] = rows_ref[:, 0:MLA_KV_LORA]
    nlat[0:dec_t, MLA_KV_LORA:MLA_LAT] = rows_ref[:, MLA_KV_LORA:MLA_LAT]
    nkv[0:dec_t, :] = kv_ref[...]
    new_k = nlat[...].astype(BF16)
    rr = lax.broadcasted_iota(jnp.int32, (MLA_HEADS, dec_t, LANES), 1).reshape(mrows, LANES)
    cc = lax.broadcasted_iota(jnp.int32, (mrows, LANES), 1)
    s = jnp.where(cc <= rr, _dot_nt(qm, new_k), MASK_VALUE)
    _, l_m, acc_m = _online_step(s, new_k[:, 0:MLA_KV_LORA], *st_m)
    o_m = acc_m * (1.0 / l_m)
    for hd in range(MLA_HEADS):
        om_ref[:, hd * LANES:(hd + 1) * LANES] = o_m[hd * dec_t:(hd + 1) * dec_t, :]

    lam = _diff_lambda(lq1_ref[...], lk1_ref[...], lq2_ref[...], lk2_ref[...], lam_init)
    for kvh in range(DIFF_KV_HEADS):
        kd = nkv[:, kvh * LANES:(kvh + 1) * LANES].astype(BF16)
        vd = nkv[:, (2 + kvh) * LANES:(3 + kvh) * LANES].astype(BF16)
        s = _dot_nt(qd[kvh], kd) + biasn_ref[kvh]
        _, l_d, acc_d = _online_step(s, vd, *st_d[kvh])
        o = acc_d * (1.0 / l_d)
        for g in range(DIFF_GROUP):
            o0 = o[(2 * g) * dec_t:(2 * g + 1) * dec_t, :]
            o1 = o[(2 * g + 1) * dec_t:(2 * g + 2) * dec_t, :]
            od_ref[:, (kvh * 2 + g) * LANES:(kvh * 2 + g + 1) * LANES] = _diff_combine(
                o0, o1, lam, subln_ref[...], lam_init)


def _decode(page_table, qm, qd, rows, kv, bias_d, bias_n, wl, cache_mla, cache_diff,
            lam_init, layer, dec_b, dec_t):
    n_pages = page_table.shape[1]
    assert n_pages % DEC_PAGES == 0 and (n_pages // DEC_PAGES) % 2 == 0
    n_chunks = n_pages // DEC_PAGES
    n_pool = cache_diff.shape[1]
    cd = cache_diff.reshape(cache_diff.shape[0], n_pool, PAGE_SIZE * 4, LANES)
    cm = jnp.swapaxes(cache_mla, 2, 3)
    const = lambda b, pt: (0, 0)
    small = lambda a: pl.BlockSpec(a.shape, const)
    grid_spec = pltpu.PrefetchScalarGridSpec(
        num_scalar_prefetch=1,
        grid=(dec_b,),
        in_specs=[
            pl.BlockSpec((MLA_HEADS, dec_t, QPAD), lambda b, pt: (0, b, 0)),
            pl.BlockSpec((8, dec_t, LANES), lambda b, pt: (0, b, 0)),
            pl.BlockSpec((dec_t, MLA_LAT), lambda b, pt: (b, 0)),
            pl.BlockSpec((dec_t, 512), lambda b, pt: (b, 0)),
            pl.BlockSpec(bias_d.shape, lambda b, pt: (0, 0, 0, 0)),
            pl.BlockSpec(bias_n.shape, lambda b, pt: (0, 0, 0)),
            small(wl['lq1']), small(wl['lk1']), small(wl['lq2']), small(wl['lk2']), small(wl['subln_g']),
            pl.BlockSpec(memory_space=pl.ANY),
            pl.BlockSpec(memory_space=pl.ANY),
        ],
        out_specs=(
            pl.BlockSpec((dec_t, MLA_HEADS * MLA_KV_LORA), lambda b, pt: (b, 0)),
            pl.BlockSpec((dec_t, DIFF_HEADS * LANES), lambda b, pt: (b, 0)),
        ),
        scratch_shapes=[
            pltpu.VMEM((2, MLA_LAT, DEC_TK), F32),
            pltpu.VMEM((2, DEC_TK * 4, LANES), F32),
            pltpu.VMEM((LANES, QPAD), F32),
            pltpu.VMEM((LANES, 4 * LANES), F32),
            pltpu.SemaphoreType.DMA((2, 2)),
        ],
    )
    return pl.pallas_call(
        functools.partial(_decode_kernel, lam_init=lam_init, layer=layer, n_chunks=n_chunks, dec_t=dec_t),
        out_shape=(
            jax.ShapeDtypeStruct((dec_b * dec_t, MLA_HEADS * MLA_KV_LORA), F32),
            jax.ShapeDtypeStruct((dec_b * dec_t, DIFF_HEADS * LANES), F32),
        ),
        grid_spec=grid_spec,
        compiler_params=_cparams("arbitrary"),
        name="paged_decode",
    )(page_table, qm, qd, rows, kv, bias_d, bias_n,
      wl['lq1'], wl['lk1'], wl['lq2'], wl['lk2'], wl['subln_g'], cm, cd)


def _merge_kernel(x_ref, om_ref, od_ref, pre_g_ref, wg_ref, wuv_ref, woa_ref, wob_ref, wout_ref,
                  post_g_ref, o_ref):
    x = x_ref[...]
    h = _rms(x, pre_g_ref[...]).astype(BF16)
    gates = _dot(h, wg_ref[...])
    v_a = _dot(om_ref[...].astype(BF16), wuv_ref[...])
    y_a = _dot(v_a.astype(BF16), woa_ref[...])
    y_b = _dot(od_ref[...].astype(BF16), wob_ref[...])
    z = jax.nn.sigmoid(gates[:, 0:D_MODEL]) * y_a + jax.nn.sigmoid(gates[:, D_MODEL:2 * D_MODEL]) * y_b
    y = _dot(z.astype(BF16), wout_ref[...])
    o_ref[...] = x + _rms(y, post_g_ref[...])


def _merge(x, o_m, o_d, wl, tm):
    n = x.shape[0]
    const = lambda i: (0, 0)
    w = lambda a: pl.BlockSpec(a.shape, const)
    row = lambda width: pl.BlockSpec((tm, width), lambda i: (i, 0))
    return pl.pallas_call(
        _merge_kernel,
        out_shape=jax.ShapeDtypeStruct((n, D_MODEL), F32),
        grid=(n // tm,),
        in_specs=[row(D_MODEL), row(o_m.shape[1]), row(o_d.shape[1]),
                  w(wl['mix_pre_g']), w(wl['w_gates']), w(wl['w_uv_bd']), w(wl['w_o_mla']),
                  w(wl['w_o_diff']), w(wl['w_out']), w(wl['mix_post_g'])],
        out_specs=row(D_MODEL),
        compiler_params=_cparams("parallel"),
        name="merge",
    )(x, o_m, o_d, wl['mix_pre_g'], wl['w_gates'], wl['w_uv_bd'], wl['w_o_mla'], wl['w_o_diff'],
      wl['w_out'], wl['mix_post_g'])


def _rope_tables(pos, n_rows):
    half = MLA_ROPE // 2
    inv_freq = ROPE_THETA ** (-jnp.arange(half, dtype=F32) / half)
    ang = pos.astype(F32)[:, None] * inv_freq[None, :]
    cos, sin = jnp.cos(ang), jnp.sin(ang)
    zeros = jnp.zeros((pos.shape[0], LANES - MLA_ROPE), F32)
    cos_t = jnp.concatenate([cos, cos, zeros], axis=1)
    sin_t = jnp.concatenate([-sin, sin, zeros], axis=1)
    reps = n_rows // pos.shape[0]
    return jnp.tile(cos_t, (reps, 1)), jnp.tile(sin_t, (reps, 1))


def _layer_weights(layer, p):
    g = lambda name: p[name][layer].reshape(1, -1)
    w_in = p['w_in'][layer]
    o = np.cumsum((MLA_Q_LORA, MLA_KV_LORA, MLA_ROPE, DIFF_HEADS * 2 * DIFF_DH,
                   DIFF_KV_HEADS * 2 * DIFF_DH, DIFF_KV_HEADS * 2 * DIFF_DH)).tolist()
    half = MLA_ROPE // 2
    kr = w_in[:, o[1]:o[2]]
    zpad = jnp.zeros((D_MODEL, LANES - MLA_ROPE), F32)
    kr_a = jnp.concatenate([kr, zpad], axis=1)
    kr_b = jnp.concatenate([kr[:, half:], kr[:, :half], zpad], axis=1)
    w_in_a = jnp.concatenate([w_in[:, :o[1]], kr_a, kr_b, w_in[:, o[2]:o[5]]], axis=1).astype(BF16)

    w_uq = p['w_uq'][layer].reshape(MLA_Q_LORA, MLA_HEADS, MLA_NOPE + MLA_ROPE)
    wq_nope = jnp.moveaxis(w_uq[:, :, :MLA_NOPE], 1, 0).astype(BF16)
    wk_t = jnp.transpose(p['w_uk'][layer], (1, 2, 0)).astype(BF16)
    x1 = w_uq[:, :, MLA_NOPE:MLA_NOPE + half]
    x2 = w_uq[:, :, MLA_NOPE + half:]
    hz = jnp.zeros((MLA_Q_LORA, MLA_HEADS, LANES - MLA_ROPE), F32)
    w_rope_a = jnp.concatenate([x1, x2, hz], axis=2).reshape(MLA_Q_LORA, MLA_HEADS * LANES).astype(BF16)
    w_rope_b = jnp.concatenate([x2, x1, hz], axis=2).reshape(MLA_Q_LORA, MLA_HEADS * LANES).astype(BF16)

    eye = jnp.eye(MLA_HEADS, dtype=F32)
    w_uv_bd = jnp.einsum('lhv,hg->hlgv', p['w_uv'][layer], eye).reshape(
        MLA_HEADS * MLA_KV_LORA, MLA_HEADS * MLA_V).astype(BF16)
    return {
        'ffn1_pre_g': g('ffn1_pre_g'), 'ffn1_w_gu': p['ffn1_w_gu'][layer].astype(BF16),
        'ffn1_w_down': p['ffn1_w_down'][layer].astype(BF16), 'ffn1_post_g': g('ffn1_post_g'),
        'ffn2_pre_g': g('ffn2_pre_g'), 'ffn2_w_gu': p['ffn2_w_gu'][layer].astype(BF16),
        'ffn2_w_down': p['ffn2_w_down'][layer].astype(BF16), 'ffn2_post_g': g('ffn2_post_g'),
        'mix_pre_g': g('mix_pre_g'), 'w_in_a': w_in_a, 'w_gates': w_in[:, o[5]:].astype(BF16),
        'q_norm_g': g('mla_q_norm_g'), 'kv_norm_g': g('mla_kv_norm_g'),
        'w_abs': _absorb(wq_nope, wk_t), 'w_rope_a': w_rope_a, 'w_rope_b': w_rope_b,
        'w_uv_bd': w_uv_bd, 'w_o_mla': p['w_o_mla'][layer].astype(BF16),
        'w_o_diff': p['w_o_diff'][layer].astype(BF16), 'w_out': p['w_out'][layer].astype(BF16),
        'mix_post_g': g('mix_post_g'),
        'lq1': g('diff_lq1'), 'lk1': g('diff_lk1'), 'lq2': g('diff_lq2'), 'lk2': g('diff_lk2'),
        'subln_g': g('diff_subln_g'),
    }


def _far_rows(rel_table, rows_per_map):
    far = rel_table[REL_BUCKETS - 1].astype(F32).reshape(DIFF_KV_HEADS, DIFF_GROUP, 1, 1)
    return jnp.broadcast_to(far, (DIFF_KV_HEADS, DIFF_GROUP, 2, rows_per_map)).reshape(
        DIFF_KV_HEADS, 4 * rows_per_map, 1)


def kernel(x_prompt, x_sample, cache_mla, cache_diff, page_table, ffn1_pre_g, ffn1_w_gu, ffn1_w_down, ffn1_post_g, mix_pre_g, w_in, mla_q_norm_g, w_uq, mla_kv_norm_g, w_uk, w_uv, diff_lq1, diff_lk1, diff_lq2, diff_lk2, diff_subln_g, rel_table, w_o_mla, w_o_diff, w_out, mix_post_g, ffn2_pre_g, ffn2_w_gu, ffn2_w_down, ffn2_post_g):
    params = dict(
        ffn1_pre_g=ffn1_pre_g, ffn1_w_gu=ffn1_w_gu, ffn1_w_down=ffn1_w_down, ffn1_post_g=ffn1_post_g,
        mix_pre_g=mix_pre_g, w_in=w_in, mla_q_norm_g=mla_q_norm_g, w_uq=w_uq,
        mla_kv_norm_g=mla_kv_norm_g, w_uk=w_uk, w_uv=w_uv, diff_lq1=diff_lq1, diff_lk1=diff_lk1,
        diff_lq2=diff_lq2, diff_lk2=diff_lk2, diff_subln_g=diff_subln_g, w_o_mla=w_o_mla,
        w_o_diff=w_o_diff, w_out=w_out, mix_post_g=mix_post_g, ffn2_pre_g=ffn2_pre_g,
        ffn2_w_gu=ffn2_w_gu, ffn2_w_down=ffn2_w_down, ffn2_post_g=ffn2_post_g)
    batch, seq, _ = x_prompt.shape
    dec_b, dec_t, _ = x_sample.shape
    depth = cache_mla.shape[0]
    past_len = page_table.shape[1] * PAGE_SIZE
    n_p, n_s = batch * seq, dec_b * dec_t
    tm_p, tm_s = 512, 512

    cos_p, sin_p = _rope_tables(jnp.arange(seq), seq)
    cos_s, sin_s = _rope_tables(past_len + jnp.arange(dec_t), tm_s)
    bias_p, bias_d, bias_n = _rel_bias_tiles(rel_table.astype(F32), past_len, dec_t)
    far_p = _far_rows(rel_table, PROMPT_TQ)

    y_p = x_prompt.reshape(n_p, D_MODEL)
    y_s = x_sample.reshape(n_s, D_MODEL)
    outs = ([], [], [], [])
    for layer in range(depth):
        wl = _layer_weights(layer, params)
        lam_init = 0.8 - 0.6 * math.exp(-0.3 * layer)

        y_p = _ffn(y_p, wl['ffn1_pre_g'], wl['ffn1_w_gu'], wl['ffn1_w_down'], wl['ffn1_post_g'], 1024)
        q, lat, rows_p, qd, kd, vd, kv_p = _mix_in(y_p, wl, cos_p, sin_p, tm_p, BF16)
        o_m = _mla_prompt(q, lat, batch, seq)
        o_d = _diff_prompt(qd, kd, vd, bias_p, far_p, wl, lam_init, batch, seq)
        y_p = _merge(y_p, o_m, o_d, wl, tm_p)
        y_p = _ffn(y_p, wl['ffn2_pre_g'], wl['ffn2_w_gu'], wl['ffn2_w_down'], wl['ffn2_post_g'], 1024)

        y_s = _ffn(y_s, wl['ffn1_pre_g'], wl['ffn1_w_gu'], wl['ffn1_w_down'], wl['ffn1_post_g'], tm_s)
        q, _, rows_s, qd, _, _, kv_s = _mix_in(y_s, wl, cos_s, sin_s, tm_s, F32)
        o_m, o_d = _decode(page_table, q, qd, rows_s, kv_s, bias_d, bias_n, wl,
                           cache_mla, cache_diff, lam_init, layer, dec_b, dec_t)
        y_s = _merge(y_s, o_m, o_d, wl, tm_s)
        y_s = _ffn(y_s, wl['ffn2_pre_g'], wl['ffn2_w_gu'], wl['ffn2_w_down'], wl['ffn2_post_g'], tm_s)

        outs[0].append(rows_p.reshape(batch, seq, MLA_LAT))
        outs[1].append(kv_p.reshape(batch, seq, 2, DIFF_KV_HEADS, 2 * DIFF_DH))
        outs[2].append(rows_s.reshape(dec_b, dec_t, MLA_LAT))
        outs[3].append(kv_s.reshape(dec_b, dec_t, 2, DIFF_KV_HEADS, 2 * DIFF_DH))
    return (y_p.reshape(batch, seq, D_MODEL), y_s.reshape(dec_b, dec_t, D_MODEL),
            jnp.stack(outs[0]), jnp.stack(outs[1]), jnp.stack(outs[2]), jnp.stack(outs[3]))
```

```python
import functools
import math

import numpy as np
import jax
import jax.numpy as jnp
from jax import lax
from jax.experimental import pallas as pl
from jax.experimental.pallas import tpu as pltpu

F32 = jnp.float32
BF16 = jnp.bfloat16

D_MODEL = 1024
MLA_HEADS = 8
MLA_NOPE = 64
MLA_ROPE = 32
MLA_V = 64
MLA_Q_LORA = D_MODEL // 4
MLA_KV_LORA = D_MODEL // 8
MLA_LAT = MLA_KV_LORA + MLA_ROPE
MLA_SCALE = (MLA_NOPE + MLA_ROPE) ** -0.5
DIFF_HEADS = 4
DIFF_KV_HEADS = 2
DIFF_GROUP = DIFF_HEADS // DIFF_KV_HEADS
DIFF_DH = 64
DIFF_SCALE = DIFF_DH ** -0.5
REL_BUCKETS = 32
REL_MAX_DIST = 128
D_FF = ((8 * D_MODEL // 3 + 127) // 128) * 128
ROPE_THETA = 10000.0
NORM_EPS = 1e-6
MASK_VALUE = -1e30
PAGE_SIZE = 128
LOG2E = math.log2(math.e)
MLA_QSCALE = MLA_SCALE * LOG2E
DIFF_QSCALE = DIFF_SCALE * LOG2E

LANES = 128
QPAD = 2 * LANES
VMEM_LIMIT = 56 * 1024 * 1024

FFN_TF = 256
PROMPT_TQ = 256
PROMPT_TK = 256
DEC_PAGES = 16
DEC_TK = DEC_PAGES * PAGE_SIZE


def _bucket_thresholds():
    max_exact = REL_BUCKETS // 2
    n = np.arange(0, 4 * REL_MAX_DIST)
    large = max_exact + (np.log(np.maximum(n, 1).astype(np.float32) / max_exact)
                         / math.log(REL_MAX_DIST / max_exact)
                         * (REL_BUCKETS - max_exact)).astype(np.int32)
    bucket = np.where(n < max_exact, n, np.minimum(large, REL_BUCKETS - 1))
    assert np.all(np.diff(bucket) >= 0)
    return [int(np.argmax(bucket >= k)) for k in range(REL_BUCKETS)]


_BUCKET_THR = _bucket_thresholds()
REL_FAR = _BUCKET_THR[REL_BUCKETS - 1]


def _rms(x, g):
    return x * lax.rsqrt(jnp.mean(x * x, axis=-1, keepdims=True) + NORM_EPS) * g


def _dot(a, b):
    return jnp.dot(a, b, preferred_element_type=F32)


def _dot_nt(a, b):
    return lax.dot_general(a, b, (((1,), (1,)), ((), ())), preferred_element_type=F32)


def _cparams(*sem):
    return pltpu.CompilerParams(dimension_semantics=sem, vmem_limit_bytes=VMEM_LIMIT)


def _ffn_kernel(x_ref, pre_g_ref, wg_ref, wu_ref, wd_ref, post_g_ref, o_ref, h_ref, acc_ref):
    f = pl.program_id(1)

    @pl.when(f == 0)
    def _():
        h_ref[...] = _rms(x_ref[...], pre_g_ref[...]).astype(BF16)
        acc_ref[...] = jnp.zeros_like(acc_ref)

    h = h_ref[...]
    gate = _dot(h, wg_ref[...])
    up = _dot(h, wu_ref[...])
    act = (gate * jax.nn.sigmoid(gate) * up).astype(BF16)
    acc_ref[...] += _dot(act, wd_ref[...])

    @pl.when(f == pl.num_programs(1) - 1)
    def _():
        o_ref[...] = x_ref[...] + 0.5 * _rms(acc_ref[...], post_g_ref[...])


def _ffn(x, pre_g, w_gu, w_down, post_g, tm):
    n = x.shape[0]
    nf = D_FF // FFN_TF
    return pl.pallas_call(
        _ffn_kernel,
        out_shape=jax.ShapeDtypeStruct((n, D_MODEL), F32),
        grid=(n // tm, nf),
        in_specs=[
            pl.BlockSpec((tm, D_MODEL), lambda i, f: (i, 0)),
            pl.BlockSpec((1, D_MODEL), lambda i, f: (0, 0)),
            pl.BlockSpec((D_MODEL, FFN_TF), lambda i, f: (0, f)),
            pl.BlockSpec((D_MODEL, FFN_TF), lambda i, f: (0, f + D_FF // FFN_TF)),
            pl.BlockSpec((FFN_TF, D_MODEL), lambda i, f: (f, 0)),
            pl.BlockSpec((1, D_MODEL), lambda i, f: (0, 0)),
        ],
        out_specs=pl.BlockSpec((tm, D_MODEL), lambda i, f: (i, 0)),
        scratch_shapes=[pltpu.VMEM((tm, D_MODEL), BF16), pltpu.VMEM((tm, D_MODEL), F32)],
        compiler_params=_cparams("parallel", "arbitrary"),
        name="ffn_half",
    )(x, pre_g, w_gu, w_gu, w_down, post_g)


def _absorb_kernel(wq_ref, wk_ref, o_ref):
    for h in range(MLA_HEADS):
        o_ref[:, h * LANES:(h + 1) * LANES] = _dot(wq_ref[h], wk_ref[h]).astype(BF16)


def _absorb(wq_nope, wk_t):
    return pl.pallas_call(
        _absorb_kernel,
        out_shape=jax.ShapeDtypeStruct((MLA_Q_LORA, MLA_HEADS * MLA_KV_LORA), BF16),
        name="absorb_uk",
    )(wq_nope, wk_t)


def _bias_from_dist(dist, table_ref, head):
    val = jnp.full(dist.shape, table_ref[0, head] * LOG2E, F32)
    for k in range(1, REL_BUCKETS):
        val = jnp.where(dist >= _BUCKET_THR[k], table_ref[k, head] * LOG2E, val)
    return jnp.where(dist < 0, MASK_VALUE, val)


def _bias_kernel(table_ref, bp_ref, bd_ref, bn_ref, *, past_len, dec_t):
    r = lax.broadcasted_iota(jnp.int32, (PROMPT_TQ, PROMPT_TK), 0)
    c = lax.broadcasted_iota(jnp.int32, (PROMPT_TQ, PROMPT_TK), 1)
    for head in range(DIFF_HEADS):
        for kind in range(2):
            bp_ref[head, kind] = _bias_from_dist(r - c + kind * PROMPT_TK, table_ref, head)
    t = lax.broadcasted_iota(jnp.int32, (dec_t, DEC_TK), 0)
    c = lax.broadcasted_iota(jnp.int32, (dec_t, DEC_TK), 1)
    tn = lax.broadcasted_iota(jnp.int32, (dec_t, LANES), 0)
    cn = lax.broadcasted_iota(jnp.int32, (dec_t, LANES), 1)
    for kvh in range(DIFF_KV_HEADS):
        for g in range(DIFF_GROUP):
            head = kvh * DIFF_GROUP + g
            far = _bias_from_dist(t - c + past_len, table_ref, head)
            last = _bias_from_dist(t - c + DEC_TK, table_ref, head)
            new = jnp.where(cn < dec_t, _bias_from_dist(tn - cn, table_ref, head), MASK_VALUE)
            for m in range(2):
                row = (g * 2 + m) * dec_t
                bd_ref[0, kvh, row:row + dec_t, :] = far
                bd_ref[1, kvh, row:row + dec_t, :] = last
                bn_ref[kvh, row:row + dec_t, :] = new


def _rel_bias_tiles(rel_table, past_len, dec_t):
    assert past_len - DEC_TK >= REL_FAR and dec_t <= LANES
    return pl.pallas_call(
        functools.partial(_bias_kernel, past_len=past_len, dec_t=dec_t),
        out_shape=(
            jax.ShapeDtypeStruct((DIFF_HEADS, 2, PROMPT_TQ, PROMPT_TK), F32),
            jax.ShapeDtypeStruct((2, DIFF_KV_HEADS, 4 * dec_t, DEC_TK), F32),
            jax.ShapeDtypeStruct((DIFF_KV_HEADS, 4 * dec_t, LANES), F32),
        ),
        in_specs=[pl.BlockSpec(memory_space=pltpu.SMEM)],
        name="rel_bias_tiles",
    )(rel_table)


def _mix_in_kernel(x_ref, pre_g_ref, w_in_ref, qg_ref, w_abs_ref, w_ra_ref, w_rb_ref, kvg_ref,
                   cos_ref, sin_ref,
                   q_ref, lat_t_ref, vx_ref, rows_ref, qd_ref, kd_t_ref, vdx_ref, kv_ref):
    tm = x_ref.shape[0]
    tk = PROMPT_TK
    h = _rms(x_ref[...], pre_g_ref[...]).astype(BF16)
    p = _dot(h, w_in_ref[...])
    cq = p[:, 0:256]
    ckv = p[:, 256:384]
    kr_a = p[:, 384:512]
    kr_b = p[:, 512:640]
    dq = p[:, 640:1152]
    dk = p[:, 1152:1408]
    dv = p[:, 1408:1664]
    cos = cos_ref[...]
    sin = sin_ref[...]

    cqn = _rms(cq, qg_ref[...]).astype(BF16)
    q_lat = _dot(cqn, w_abs_ref[...])
    r_a = _dot(cqn, w_ra_ref[...])
    r_b = _dot(cqn, w_rb_ref[...])
    for hd in range(MLA_HEADS):
        sl = slice(hd * LANES, (hd + 1) * LANES)
        q_ref[hd, :, 0:LANES] = (q_lat[:, sl] * MLA_QSCALE).astype(q_ref.dtype)
        q_ref[hd, :, LANES:QPAD] = ((r_a[:, sl] * cos + r_b[:, sl] * sin) * MLA_QSCALE).astype(q_ref.dtype)

    ckvn = _rms(ckv, kvg_ref[...])
    k_rope = kr_a * cos + kr_b * sin
    rows_ref[:, 0:MLA_KV_LORA] = ckvn
    rows_ref[:, MLA_KV_LORA:MLA_LAT] = k_rope[:, 0:MLA_ROPE]
    for c in range(tm // tk):
        rs = slice(c * tk, (c + 1) * tk)
        lat_t_ref[c, 0:LANES, :] = ckvn[rs].T.astype(BF16)
        lat_t_ref[c, LANES:QPAD, :] = k_rope[rs].T.astype(BF16)
        for kvh in range(DIFF_KV_HEADS):
            kd_t_ref[kvh, c] = dk[rs, kvh * LANES:(kvh + 1) * LANES].T.astype(BF16)
    ones = jnp.ones((tm, LANES), BF16)
    vx_ref[:, 0:LANES] = ckvn.astype(BF16)
    vx_ref[:, LANES:QPAD] = ones
    for kvh in range(DIFF_KV_HEADS):
        vdx_ref[:, 2 * kvh * LANES:(2 * kvh + 1) * LANES] = dv[:, kvh * LANES:(kvh + 1) * LANES].astype(BF16)
        vdx_ref[:, (2 * kvh + 1) * LANES:(2 * kvh + 2) * LANES] = ones

    lane = lax.broadcasted_iota(jnp.int32, (tm, LANES), 1)
    for kvh in range(DIFF_KV_HEADS):
        for g in range(DIFF_GROUP):
            pair = dq[:, (kvh * 2 + g) * LANES:(kvh * 2 + g + 1) * LANES] * DIFF_QSCALE
            for m in range(2):
                keep = (lane < DIFF_DH) if m == 0 else (lane >= DIFF_DH)
                qd_ref[kvh * 4 + g * 2 + m] = jnp.where(keep, pair, 0.0).astype(qd_ref.dtype)
    kv_ref[:, 0:256] = dk
    kv_ref[:, 256:512] = dv


def _mix_in(x, wl, cos_t, sin_t, tm, q_dtype):
    n = x.shape[0]
    n_tab = cos_t.shape[0] // tm
    tk = PROMPT_TK
    const = lambda i: (0, 0)
    w = lambda a: pl.BlockSpec(a.shape, const)
    return pl.pallas_call(
        _mix_in_kernel,
        out_shape=(
            jax.ShapeDtypeStruct((MLA_HEADS, n, QPAD), q_dtype),
            jax.ShapeDtypeStruct((n // tk, QPAD, tk), BF16),
            jax.ShapeDtypeStruct((n, QPAD), BF16),
            jax.ShapeDtypeStruct((n, MLA_LAT), F32),
            jax.ShapeDtypeStruct((8, n, LANES), q_dtype),
            jax.ShapeDtypeStruct((DIFF_KV_HEADS, n // tk, LANES, tk), BF16),
            jax.ShapeDtypeStruct((n, 2 * DIFF_KV_HEADS * LANES), BF16),
            jax.ShapeDtypeStruct((n, 512), F32),
        ),
        grid=(n // tm,),
        in_specs=[
            pl.BlockSpec((tm, D_MODEL), lambda i: (i, 0)),
            w(wl['mix_pre_g']), w(wl['w_in_a']), w(wl['q_norm_g']), w(wl['w_abs']),
            w(wl['w_rope_a']), w(wl['w_rope_b']), w(wl['kv_norm_g']),
            pl.BlockSpec((tm, LANES), lambda i: (i % n_tab, 0)),
            pl.BlockSpec((tm, LANES), lambda i: (i % n_tab, 0)),
        ],
        out_specs=(
            pl.BlockSpec((MLA_HEADS, tm, QPAD), lambda i: (0, i, 0)),
            pl.BlockSpec((tm // tk, QPAD, tk), lambda i: (i, 0, 0)),
            pl.BlockSpec((tm, QPAD), lambda i: (i, 0)),
            pl.BlockSpec((tm, MLA_LAT), lambda i: (i, 0)),
            pl.BlockSpec((8, tm, LANES), lambda i: (0, i, 0)),
            pl.BlockSpec((DIFF_KV_HEADS, tm // tk, LANES, tk), lambda i: (0, i, 0, 0)),
            pl.BlockSpec((tm, 2 * DIFF_KV_HEADS * LANES), lambda i: (i, 0)),
            pl.BlockSpec((tm, 512), lambda i: (i, 0)),
        ),
        compiler_params=_cparams("parallel"),
        name="mix_in",
    )(x, wl['mix_pre_g'], wl['w_in_a'], wl['q_norm_g'], wl['w_abs'], wl['w_rope_a'], wl['w_rope_b'],
      wl['kv_norm_g'], cos_t, sin_t)


def _online_step(s, v, m, l, acc, shift=None, v_transposed=False):
    m_blk = jnp.max(s, axis=-1, keepdims=True)
    if shift is not None:
        m_blk = m_blk + shift
    m_new = jnp.maximum(m, m_blk)
    alpha = jnp.exp2(m - m_new)
    p = jnp.exp2(s - (m_new if shift is None else m_new - shift))
    l_new = alpha * l + jnp.sum(p, axis=-1, keepdims=True)
    pv = _dot_nt(p.astype(BF16), v) if v_transposed else _dot(p.astype(BF16), v)
    acc_new = alpha * acc + pv
    return m_new, l_new, acc_new


def _flash_step(s, vx, m_ref, acc_ref, shift=None):
    tiles = [s[:, t * LANES:(t + 1) * LANES] for t in range(s.shape[1] // LANES)]
    m_blk = jnp.max(functools.reduce(jnp.maximum, tiles), axis=-1, keepdims=True)
    m_prev = m_ref[...]
    m_new = jnp.maximum(m_prev, m_blk if shift is None else m_blk + shift)
    alpha = jnp.exp2(m_prev - m_new)
    sub = m_new if shift is None else m_new - shift
    p = jnp.concatenate([jnp.exp2(t - sub) for t in tiles], axis=1).astype(BF16)
    acc_ref[...] = jnp.concatenate([alpha, alpha], axis=1) * acc_ref[...] + _dot(p, vx)
    m_ref[...] = m_new


def _flash_init(m_ref, acc_ref):
    m_ref[...] = jnp.full_like(m_ref, -jnp.inf)
    acc_ref[...] = jnp.zeros_like(acc_ref)


def _mla_prompt_kernel(q_ref, kt_ref, vx_ref, o_ref, m_ref, acc_ref):
    i = pl.program_id(1)
    tq, tk = PROMPT_TQ, PROMPT_TK
    rows = MLA_HEADS * tq
    q = q_ref[...].reshape(rows, QPAD)
    _flash_init(m_ref, acc_ref)

    def chunk(j, mask):
        s = _dot(q, kt_ref[j])
        if mask is not None:
            s = jnp.where(mask, s, MASK_VALUE)
        _flash_step(s, vx_ref[pl.ds(pl.multiple_of(j * tk, tk), tk), :], m_ref, acc_ref)

    def body(j, carry):
        chunk(j, None)
        return carry

    lax.fori_loop(0, i, body, 0)
    r = lax.broadcasted_iota(jnp.int32, (MLA_HEADS, tq, tk), 1).reshape(rows, tk)
    c = lax.broadcasted_iota(jnp.int32, (rows, tk), 1)
    chunk(i, c <= r)

    for hd in range(MLA_HEADS):
        a = acc_ref[hd * tq:(hd + 1) * tq, :]
        o_ref[:, hd * LANES:(hd + 1) * LANES] = (a[:, 0:LANES] * (1.0 / a[:, LANES:QPAD])).astype(o_ref.dtype)


def _mla_prompt(q, lat_t, vx, batch, seq):
    tq, tk = PROMPT_TQ, PROMPT_TK
    assert tq == tk
    nq = seq // tq
    rows = MLA_HEADS * tq
    return pl.pallas_call(
        _mla_prompt_kernel,
        out_shape=jax.ShapeDtypeStruct((batch * seq, MLA_HEADS * MLA_KV_LORA), BF16),
        grid=(batch, nq),
        in_specs=[
            pl.BlockSpec((MLA_HEADS, tq, QPAD), lambda b, i: (0, b * nq + i, 0)),
            pl.BlockSpec((seq // tk, QPAD, tk), lambda b, i: (b, 0, 0)),
            pl.BlockSpec((seq, QPAD), lambda b, i: (b, 0)),
        ],
        out_specs=pl.BlockSpec((tq, MLA_HEADS * MLA_KV_LORA), lambda b, i: (b * nq + i, 0)),
        scratch_shapes=[pltpu.VMEM((rows, LANES), F32), pltpu.VMEM((rows, QPAD), F32)],
        compiler_params=_cparams("parallel", "arbitrary"),
        name="mla_prompt",
    )(q, lat_t, vx)


def _diff_lambda(lq1, lk1, lq2, lk2, lam_init):
    return (jnp.exp(jnp.sum(lq1 * lk1, axis=-1, keepdims=True))
            - jnp.exp(jnp.sum(lq2 * lk2, axis=-1, keepdims=True)) + lam_init)


def _diff_combine(o0, o1, lam, subln_g, lam_init):
    return _rms(o0 - lam * o1, subln_g) * (1.0 - lam_init)


def _diff_prompt_kernel(q_ref, kt_ref, vx_ref, bias_ref, far_ref, lq1_ref, lk1_ref, lq2_ref, lk2_ref,
                        subln_ref, o_ref, m_ref, acc_ref, *, lam_init):
    i = pl.program_id(2)
    tq, tk = PROMPT_TQ, PROMPT_TK
    rows = 4 * tq
    q = q_ref[...].reshape(rows, LANES)
    _flash_init(m_ref, acc_ref)

    def chunk(j, kind):
        s = _dot(q, kt_ref[j])
        shift = far_ref[...]
        if kind is not None:
            s = (s.reshape(DIFF_GROUP, 2, tq, tk) + bias_ref[:, kind][:, None]).reshape(rows, tk)
            shift = None
        _flash_step(s, vx_ref[pl.ds(pl.multiple_of(j * tk, tk), tk), :], m_ref, acc_ref, shift)

    def body(j, carry):
        chunk(j, None)
        return carry

    lax.fori_loop(0, jnp.maximum(i - 1, 0), body, 0)

    @pl.when(i > 0)
    def _():
        chunk(i - 1, 1)

    chunk(i, 0)

    lam = _diff_lambda(lq1_ref[...], lk1_ref[...], lq2_ref[...], lk2_ref[...], lam_init)
    acc = acc_ref[...]
    o = acc[:, 0:LANES] * (1.0 / acc[:, LANES:2 * LANES])
    for g in range(DIFF_GROUP):
        o0 = o[(2 * g) * tq:(2 * g + 1) * tq, :]
        o1 = o[(2 * g + 1) * tq:(2 * g + 2) * tq, :]
        o_ref[:, g * LANES:(g + 1) * LANES] = _diff_combine(
            o0, o1, lam, subln_ref[...], lam_init).astype(o_ref.dtype)


def _diff_prompt(qd, kd_t, vdx, bias_p, far_rows, wl, lam_init, batch, seq):
    tq, tk = PROMPT_TQ, PROMPT_TK
    nq = seq // tq
    rows = 4 * tq
    assert tq == tk and tk >= REL_FAR
    const3 = lambda b, h, i: (0, 0)
    small = lambda a: pl.BlockSpec(a.shape, const3)
    return pl.pallas_call(
        functools.partial(_diff_prompt_kernel, lam_init=lam_init),
        out_shape=jax.ShapeDtypeStruct((batch * seq, DIFF_HEADS * 2 * DIFF_DH), BF16),
        grid=(batch, DIFF_KV_HEADS, nq),
        in_specs=[
            pl.BlockSpec((4, tq, LANES), lambda b, h, i: (h, b * nq + i, 0)),
            pl.BlockSpec((None, seq // tk, LANES, tk), lambda b, h, i: (h, b, 0, 0)),
            pl.BlockSpec((seq, 2 * LANES), lambda b, h, i: (b, h)),
            pl.BlockSpec((DIFF_GROUP, 2, tq, tk), lambda b, h, i: (h, 0, 0, 0)),
            pl.BlockSpec((None, rows, LANES), lambda b, h, i: (h, 0, 0)),
            small(wl['lq1']), small(wl['lk1']), small(wl['lq2']), small(wl['lk2']), small(wl['subln_g']),
        ],
        out_specs=pl.BlockSpec((tq, DIFF_GROUP * LANES), lambda b, h, i: (b * nq + i, h)),
        scratch_shapes=[pltpu.VMEM((rows, LANES), F32), pltpu.VMEM((rows, 2 * LANES), F32)],
        compiler_params=_cparams("parallel", "parallel", "arbitrary"),
        name="diff_prompt",
    )(qd, kd_t, vdx, bias_p, far_rows, wl['lq1'], wl['lk1'], wl['lq2'], wl['lk2'], wl['subln_g'])


def _decode_kernel(pt_ref, qm_ref, qd_ref, rows_ref, kv_ref, bias_ref, biasn_ref,
                   lq1_ref, lk1_ref, lq2_ref, lk2_ref, subln_ref, cm_hbm, cd_hbm,
                   om_ref, od_ref, mbuf, dbuf, nlat, nkv, sem,
                   *, lam_init, layer, n_chunks, dec_t):
    b = pl.program_id(0)
    nb = pl.num_programs(0)
    mrows = MLA_HEADS * dec_t
    drows = 4 * dec_t
    prow = PAGE_SIZE * 4

    def copies(bb, c, slot):
        out = []
        for p in range(DEC_PAGES):
            phys = pt_ref[bb, c * DEC_PAGES + p]
            out.append(pltpu.make_async_copy(cm_hbm.at[layer, phys],
                                             mbuf.at[slot, :, pl.ds(p * PAGE_SIZE, PAGE_SIZE)], sem.at[0, slot]))
            out.append(pltpu.make_async_copy(cd_hbm.at[layer, phys], dbuf.at[slot, pl.ds(p * prow, prow)],
                                             sem.at[1, slot]))
        return out

    @pl.when(b == 0)
    def _():
        nlat[...] = jnp.zeros_like(nlat)
        nkv[...] = jnp.zeros_like(nkv)
        for cp in copies(0, 0, 0):
            cp.start()

    qm = qm_ref[...].reshape(mrows, QPAD).astype(BF16)
    qd = [qd_ref[kvh * 4:(kvh + 1) * 4].reshape(drows, LANES).astype(BF16) for kvh in range(DIFF_KV_HEADS)]

    def init(r, d):
        return (jnp.full((r, 1), -jnp.inf, F32), jnp.zeros((r, 1), F32), jnp.zeros((r, d), F32))

    def body(c, carry):
        st_m, st_d = carry
        slot = c % 2

        @pl.when(c + 1 < n_chunks)
        def _():
            for cp in copies(b, c + 1, 1 - slot):
                cp.start()

        @pl.when(jnp.logical_and(c + 1 == n_chunks, b + 1 < nb))
        def _():
            for cp in copies(b + 1, 0, 1 - slot):
                cp.start()

        for cp in copies(b, c, slot):
            cp.wait()

        kt = mbuf[slot].astype(BF16)
        st_m = _online_step(_dot(qm[:, 0:MLA_LAT], kt), kt[0:MLA_KV_LORA, :], *st_m, v_transposed=True)

        kind = lax.convert_element_type(c == n_chunks - 1, jnp.int32)
        new_d = []
        for kvh in range(DIFF_KV_HEADS):
            kd = dbuf[slot, pl.ds(kvh, DEC_TK, stride=4), :].astype(BF16)
            vd = dbuf[slot, pl.ds(2 + kvh, DEC_TK, stride=4), :].astype(BF16)
            s = _dot_nt(qd[kvh], kd) + bias_ref[kind, kvh]
            new_d.append(_online_step(s, vd, *st_d[kvh]))
        return st_m, tuple(new_d)

    st_m, st_d = lax.fori_loop(
        0, n_chunks, body,
        (init(mrows, MLA_KV_LORA), tuple(init(drows, LANES) for _ in range(DIFF_KV_HEADS))))

    nlat[0:dec_t, 0:MLA_KV_LORA] = rows_ref[:, 0:MLA_KV_LORA]
    nlat[0:dec_t, MLA_KV_LORA:MLA_LAT] = rows_ref[:, MLA_KV_LORA:MLA_LAT]
    nkv[0:dec_t, :] = kv_ref[...]
    new_k = nlat[...].astype(BF16)
    rr = lax.broadcasted_iota(jnp.int32, (MLA_HEADS, dec_t, LANES), 1).reshape(mrows, LANES)
    cc = lax.broadcasted_iota(jnp.int32, (mrows, LANES), 1)
    s = jnp.where(cc <= rr, _dot_nt(qm, new_k), MASK_VALUE)
    _, l_m, acc_m = _online_step(s, new_k[:, 0:MLA_KV_LORA], *st_m)
    o_m = acc_m * (1.0 / l_m)
    for hd in range(MLA_HEADS):
        om_ref[:, hd * LANES:(hd + 1) * LANES] = o_m[hd * dec_t:(hd + 1) * dec_t, :]

    lam = _diff_lambda(lq1_ref[...], lk1_ref[...], lq2_ref[...], lk2_ref[...], lam_init)
    for kvh in range(DIFF_KV_HEADS):
        kd = nkv[:, kvh * LANES:(kvh + 1) * LANES].astype(BF16)
        vd = nkv[:, (2 + kvh) * LANES:(3 + kvh) * LANES].astype(BF16)
        s = _dot_nt(qd[kvh], kd) + biasn_ref[kvh]
        _, l_d, acc_d = _online_step(s, vd, *st_d[kvh])
        o = acc_d * (1.0 / l_d)
        for g in range(DIFF_GROUP):
            o0 = o[(2 * g) * dec_t:(2 * g + 1) * dec_t, :]
            o1 = o[(2 * g + 1) * dec_t:(2 * g + 2) * dec_t, :]
            od_ref[:, (kvh * 2 + g) * LANES:(kvh * 2 + g + 1) * LANES] = _diff_combine(
                o0, o1, lam, subln_ref[...], lam_init)


def _decode(page_table, qm, qd, rows, kv, bias_d, bias_n, wl, cache_mla, cache_diff,
            lam_init, layer, dec_b, dec_t):
    n_pages = page_table.shape[1]
    assert n_pages % DEC_PAGES == 0 and (n_pages // DEC_PAGES) % 2 == 0
    n_chunks = n_pages // DEC_PAGES
    n_pool = cache_diff.shape[1]
    cd = cache_diff.reshape(cache_diff.shape[0], n_pool, PAGE_SIZE * 4, LANES)
    cm = jnp.swapaxes(cache_mla, 2, 3)
    const = lambda b, pt: (0, 0)
    small = lambda a: pl.BlockSpec(a.shape, const)
    grid_spec = pltpu.PrefetchScalarGridSpec(
        num_scalar_prefetch=1,
        grid=(dec_b,),
        in_specs=[
            pl.BlockSpec((MLA_HEADS, dec_t, QPAD), lambda b, pt: (0, b, 0)),
            pl.BlockSpec((8, dec_t, LANES), lambda b, pt: (0, b, 0)),
            pl.BlockSpec((dec_t, MLA_LAT), lambda b, pt: (b, 0)),
            pl.BlockSpec((dec_t, 512), lambda b, pt: (b, 0)),
            pl.BlockSpec(bias_d.shape, lambda b, pt: (0, 0, 0, 0)),
            pl.BlockSpec(bias_n.shape, lambda b, pt: (0, 0, 0)),
            small(wl['lq1']), small(wl['lk1']), small(wl['lq2']), small(wl['lk2']), small(wl['subln_g']),
            pl.BlockSpec(memory_space=pl.ANY),
            pl.BlockSpec(memory_space=pl.ANY),
        ],
        out_specs=(
            pl.BlockSpec((dec_t, MLA_HEADS * MLA_KV_LORA), lambda b, pt: (b, 0)),
            pl.BlockSpec((dec_t, DIFF_HEADS * LANES), lambda b, pt: (b, 0)),
        ),
        scratch_shapes=[
            pltpu.VMEM((2, MLA_LAT, DEC_TK), F32),
            pltpu.VMEM((2, DEC_TK * 4, LANES), F32),
            pltpu.VMEM((LANES, QPAD), F32),
            pltpu.VMEM((LANES, 4 * LANES), F32),
            pltpu.SemaphoreType.DMA((2, 2)),
        ],
    )
    return pl.pallas_call(
        functools.partial(_decode_kernel, lam_init=lam_init, layer=layer, n_chunks=n_chunks, dec_t=dec_t),
        out_shape=(
            jax.ShapeDtypeStruct((dec_b * dec_t, MLA_HEADS * MLA_KV_LORA), F32),
            jax.ShapeDtypeStruct((dec_b * dec_t, DIFF_HEADS * LANES), F32),
        ),
        grid_spec=grid_spec,
        compiler_params=_cparams("arbitrary"),
        name="paged_decode",
    )(page_table, qm, qd, rows, kv, bias_d, bias_n,
      wl['lq1'], wl['lk1'], wl['lq2'], wl['lk2'], wl['subln_g'], cm, cd)


def _merge_kernel(x_ref, om_ref, od_ref, pre_g_ref, wg_ref, wuv_ref, woa_ref, wob_ref, wout_ref,
                  post_g_ref, o_ref):
    x = x_ref[...]
    h = _rms(x, pre_g_ref[...]).astype(BF16)
    gates = _dot(h, wg_ref[...])
    v_a = _dot(om_ref[...].astype(BF16), wuv_ref[...])
    y_a = _dot(v_a.astype(BF16), woa_ref[...])
    y_b = _dot(od_ref[...].astype(BF16), wob_ref[...])
    z = jax.nn.sigmoid(gates[:, 0:D_MODEL]) * y_a + jax.nn.sigmoid(gates[:, D_MODEL:2 * D_MODEL]) * y_b
    y = _dot(z.astype(BF16), wout_ref[...])
    o_ref[...] = x + _rms(y, post_g_ref[...])


def _merge(x, o_m, o_d, wl, tm):
    n = x.shape[0]
    const = lambda i: (0, 0)
    w = lambda a: pl.BlockSpec(a.shape, const)
    row = lambda width: pl.BlockSpec((tm, width), lambda i: (i, 0))
    return pl.pallas_call(
        _merge_kernel,
        out_shape=jax.ShapeDtypeStruct((n, D_MODEL), F32),
        grid=(n // tm,),
        in_specs=[row(D_MODEL), row(o_m.shape[1]), row(o_d.shape[1]),
                  w(wl['mix_pre_g']), w(wl['w_gates']), w(wl['w_uv_bd']), w(wl['w_o_mla']),
                  w(wl['w_o_diff']), w(wl['w_out']), w(wl['mix_post_g'])],
        out_specs=row(D_MODEL),
        compiler_params=_cparams("parallel"),
        name="merge",
    )(x, o_m, o_d, wl['mix_pre_g'], wl['w_gates'], wl['w_uv_bd'], wl['w_o_mla'], wl['w_o_diff'],
      wl['w_out'], wl['mix_post_g'])


def _rope_tables(pos, n_rows):
    half = MLA_ROPE // 2
    inv_freq = ROPE_THETA ** (-jnp.arange(half, dtype=F32) / half)
    ang = pos.astype(F32)[:, None] * inv_freq[None, :]
    cos, sin = jnp.cos(ang), jnp.sin(ang)
    zeros = jnp.zeros((pos.shape[0], LANES - MLA_ROPE), F32)
    cos_t = jnp.concatenate([cos, cos, zeros], axis=1)
    sin_t = jnp.concatenate([-sin, sin, zeros], axis=1)
    reps = n_rows // pos.shape[0]
    return jnp.tile(cos_t, (reps, 1)), jnp.tile(sin_t, (reps, 1))


def _layer_weights(layer, p):
    g = lambda name: p[name][layer].reshape(1, -1)
    w_in = p['w_in'][layer]
    o = np.cumsum((MLA_Q_LORA, MLA_KV_LORA, MLA_ROPE, DIFF_HEADS * 2 * DIFF_DH,
                   DIFF_KV_HEADS * 2 * DIFF_DH, DIFF_KV_HEADS * 2 * DIFF_DH)).tolist()
    half = MLA_ROPE // 2
    kr = w_in[:, o[1]:o[2]]
    zpad = jnp.zeros((D_MODEL, LANES - MLA_ROPE), F32)
    kr_a = jnp.concatenate([kr, zpad], axis=1)
    kr_b = jnp.concatenate([kr[:, half:], kr[:, :half], zpad], axis=1)
    w_in_a = jnp.concatenate([w_in[:, :o[1]], kr_a, kr_b, w_in[:, o[2]:o[5]]], axis=1).astype(BF16)

    w_uq = p['w_uq'][layer].reshape(MLA_Q_LORA, MLA_HEADS, MLA_NOPE + MLA_ROPE)
    wq_nope = jnp.moveaxis(w_uq[:, :, :MLA_NOPE], 1, 0).astype(BF16)
    wk_t = jnp.transpose(p['w_uk'][layer], (1, 2, 0)).astype(BF16)
    x1 = w_uq[:, :, MLA_NOPE:MLA_NOPE + half]
    x2 = w_uq[:, :, MLA_NOPE + half:]
    hz = jnp.zeros((MLA_Q_LORA, MLA_HEADS, LANES - MLA_ROPE), F32)
    w_rope_a = jnp.concatenate([x1, x2, hz], axis=2).reshape(MLA_Q_LORA, MLA_HEADS * LANES).astype(BF16)
    w_rope_b = jnp.concatenate([x2, x1, hz], axis=2).reshape(MLA_Q_LORA, MLA_HEADS * LANES).astype(BF16)

    eye = jnp.eye(MLA_HEADS, dtype=F32)
    w_uv_bd = jnp.einsum('lhv,hg->hlgv', p['w_uv'][layer], eye).reshape(
        MLA_HEADS * MLA_KV_LORA, MLA_HEADS * MLA_V).astype(BF16)
    return {
        'ffn1_pre_g': g('ffn1_pre_g'), 'ffn1_w_gu': p['ffn1_w_gu'][layer].astype(BF16),
        'ffn1_w_down': p['ffn1_w_down'][layer].astype(BF16), 'ffn1_post_g': g('ffn1_post_g'),
        'ffn2_pre_g': g('ffn2_pre_g'), 'ffn2_w_gu': p['ffn2_w_gu'][layer].astype(BF16),
        'ffn2_w_down': p['ffn2_w_down'][layer].astype(BF16), 'ffn2_post_g': g('ffn2_post_g'),
        'mix_pre_g': g('mix_pre_g'), 'w_in_a': w_in_a, 'w_gates': w_in[:, o[5]:].astype(BF16),
        'q_norm_g': g('mla_q_norm_g'), 'kv_norm_g': g('mla_kv_norm_g'),
        'w_abs': _absorb(wq_nope, wk_t), 'w_rope_a': w_rope_a, 'w_rope_b': w_rope_b,
        'w_uv_bd': w_uv_bd, 'w_o_mla': p['w_o_mla'][layer].astype(BF16),
        'w_o_diff': p['w_o_diff'][layer].astype(BF16), 'w_out': p['w_out'][layer].astype(BF16),
        'mix_post_g': g('mix_post_g'),
        'lq1': g('diff_lq1'), 'lk1': g('diff_lk1'), 'lq2': g('diff_lq2'), 'lk2': g('diff_lk2'),
        'subln_g': g('diff_subln_g'),
    }


def _far_rows(rel_table, rows_per_map):
    far = (rel_table[REL_BUCKETS - 1].astype(F32) * LOG2E).reshape(DIFF_KV_HEADS, DIFF_GROUP, 1, 1)
    return jnp.broadcast_to(far, (DIFF_KV_HEADS, DIFF_GROUP, 2 * rows_per_map, LANES)).reshape(
        DIFF_KV_HEADS, 4 * rows_per_map, LANES)


def kernel(x_prompt, x_sample, cache_mla, cache_diff, page_table, ffn1_pre_g, ffn1_w_gu, ffn1_w_down, ffn1_post_g, mix_pre_g, w_in, mla_q_norm_g, w_uq, mla_kv_norm_g, w_uk, w_uv, diff_lq1, diff_lk1, diff_lq2, diff_lk2, diff_subln_g, rel_table, w_o_mla, w_o_diff, w_out, mix_post_g, ffn2_pre_g, ffn2_w_gu, ffn2_w_down, ffn2_post_g):
    params = dict(
        ffn1_pre_g=ffn1_pre_g, ffn1_w_gu=ffn1_w_gu, ffn1_w_down=ffn1_w_down, ffn1_post_g=ffn1_post_g,
        mix_pre_g=mix_pre_g, w_in=w_in, mla_q_norm_g=mla_q_norm_g, w_uq=w_uq,
        mla_kv_norm_g=mla_kv_norm_g, w_uk=w_uk, w_uv=w_uv, diff_lq1=diff_lq1, diff_lk1=diff_lk1,
        diff_lq2=diff_lq2, diff_lk2=diff_lk2, diff_subln_g=diff_subln_g, w_o_mla=w_o_mla,
        w_o_diff=w_o_diff, w_out=w_out, mix_post_g=mix_post_g, ffn2_pre_g=ffn2_pre_g,
        ffn2_w_gu=ffn2_w_gu, ffn2_w_down=ffn2_w_down, ffn2_post_g=ffn2_post_g)
    batch, seq, _ = x_prompt.shape
    dec_b, dec_t, _ = x_sample.shape
    depth = cache_mla.shape[0]
    past_len = page_table.shape[1] * PAGE_SIZE
    n_p, n_s = batch * seq, dec_b * dec_t
    tm_p, tm_s = 512, 512

    cos_p, sin_p = _rope_tables(jnp.arange(seq), seq)
    cos_s, sin_s = _rope_tables(past_len + jnp.arange(dec_t), tm_s)
    bias_p, bias_d, bias_n = _rel_bias_tiles(rel_table.astype(F32), past_len, dec_t)
    far_p = _far_rows(rel_table, PROMPT_TQ)

    y_p = x_prompt.reshape(n_p, D_MODEL)
    y_s = x_sample.reshape(n_s, D_MODEL)
    outs = ([], [], [], [])
    for layer in range(depth):
        wl = _layer_weights(layer, params)
        lam_init = 0.8 - 0.6 * math.exp(-0.3 * layer)

        y_p = _ffn(y_p, wl['ffn1_pre_g'], wl['ffn1_w_gu'], wl['ffn1_w_down'], wl['ffn1_post_g'], 1024)
        q, lat_t, vx, rows_p, qd, kd_t, vdx, kv_p = _mix_in(y_p, wl, cos_p, sin_p, tm_p, BF16)
        o_m = _mla_prompt(q, lat_t, vx, batch, seq)
        o_d = _diff_prompt(qd, kd_t, vdx, bias_p, far_p, wl, lam_init, batch, seq)
        y_p = _merge(y_p, o_m, o_d, wl, tm_p)
        y_p = _ffn(y_p, wl['ffn2_pre_g'], wl['ffn2_w_gu'], wl['ffn2_w_down'], wl['ffn2_post_g'], 1024)

        y_s = _ffn(y_s, wl['ffn1_pre_g'], wl['ffn1_w_gu'], wl['ffn1_w_down'], wl['ffn1_post_g'], tm_s)
        q, _, _, rows_s, qd, _, _, kv_s = _mix_in(y_s, wl, cos_s, sin_s, tm_s, F32)
        o_m, o_d = _decode(page_table, q, qd, rows_s, kv_s, bias_d, bias_n, wl,
                           cache_mla, cache_diff, lam_init, layer, dec_b, dec_t)
        y_s = _merge(y_s, o_m, o_d, wl, tm_s)
        y_s = _ffn(y_s, wl['ffn2_pre_g'], wl['ffn2_w_gu'], wl['ffn2_w_down'], wl['ffn2_post_g'], tm_s)

        outs[0].append(rows_p.reshape(batch, seq, MLA_LAT))
        outs[1].append(kv_p.reshape(batch, seq, 2, DIFF_KV_HEADS, 2 * DIFF_DH))
        outs[2].append(rows_s.reshape(dec_b, dec_t, MLA_LAT))
        outs[3].append(kv_s.reshape(dec_b, dec_t, 2, DIFF_KV_HEADS, 2 * DIFF_DH))
    return (y_p.reshape(batch, seq, D_MODEL), y_s.reshape(dec_b, dec_t, D_MODEL),
            jnp.stack(outs[0]), jnp.stack(outs[1]), jnp.stack(outs[2]), jnp.stack(outs[3]))
```

```python
import functools
import math

import numpy as np
import jax
import jax.numpy as jnp
from jax import lax
from jax.experimental import pallas as pl
from jax.experimental.pallas import tpu as pltpu

F32 = jnp.float32
BF16 = jnp.bfloat16

D_MODEL = 1024
MLA_HEADS = 8
MLA_NOPE = 64
MLA_ROPE = 32
MLA_V = 64
MLA_Q_LORA = D_MODEL // 4
MLA_KV_LORA = D_MODEL // 8
MLA_LAT = MLA_KV_LORA + MLA_ROPE
MLA_SCALE = (MLA_NOPE + MLA_ROPE) ** -0.5
DIFF_HEADS = 4
DIFF_KV_HEADS = 2
DIFF_GROUP = DIFF_HEADS // DIFF_KV_HEADS
DIFF_DH = 64
DIFF_SCALE = DIFF_DH ** -0.5
REL_BUCKETS = 32
REL_MAX_DIST = 128
D_FF = ((8 * D_MODEL // 3 + 127) // 128) * 128
ROPE_THETA = 10000.0
NORM_EPS = 1e-6
MASK_VALUE = -1e30
PAGE_SIZE = 128
LOG2E = math.log2(math.e)
MLA_QSCALE = MLA_SCALE * LOG2E
DIFF_QSCALE = DIFF_SCALE * LOG2E

LANES = 128
QPAD = 2 * LANES
VMEM_LIMIT = 56 * 1024 * 1024

FFN_TF = 256
PROMPT_TQ = 256
PROMPT_TK = 256
DEC_PAGES = 32
DEC_TK = DEC_PAGES * PAGE_SIZE


def _bucket_thresholds():
    max_exact = REL_BUCKETS // 2
    n = np.arange(0, 4 * REL_MAX_DIST)
    large = max_exact + (np.log(np.maximum(n, 1).astype(np.float32) / max_exact)
                         / math.log(REL_MAX_DIST / max_exact)
                         * (REL_BUCKETS - max_exact)).astype(np.int32)
    bucket = np.where(n < max_exact, n, np.minimum(large, REL_BUCKETS - 1))
    assert np.all(np.diff(bucket) >= 0)
    return [int(np.argmax(bucket >= k)) for k in range(REL_BUCKETS)]


_BUCKET_THR = _bucket_thresholds()
REL_FAR = _BUCKET_THR[REL_BUCKETS - 1]


def _rms(x, g):
    return x * lax.rsqrt(jnp.mean(x * x, axis=-1, keepdims=True) + NORM_EPS) * g


def _dot(a, b):
    return jnp.dot(a, b, preferred_element_type=F32)


def _dot_nt(a, b):
    return lax.dot_general(a, b, (((1,), (1,)), ((), ())), preferred_element_type=F32)


def _cparams(*sem):
    return pltpu.CompilerParams(dimension_semantics=sem, vmem_limit_bytes=VMEM_LIMIT)


def _ffn_kernel(x_ref, pre_g_ref, wg_ref, wu_ref, wd_ref, post_g_ref, o_ref, h_ref, acc_ref):
    f = pl.program_id(1)

    @pl.when(f == 0)
    def _():
        h_ref[...] = _rms(x_ref[...], pre_g_ref[...]).astype(BF16)
        acc_ref[...] = jnp.zeros_like(acc_ref)

    h = h_ref[...]
    gate = _dot(h, wg_ref[...])
    up = _dot(h, wu_ref[...])
    act = (gate * jax.nn.sigmoid(gate) * up).astype(BF16)
    acc_ref[...] += _dot(act, wd_ref[...])

    @pl.when(f == pl.num_programs(1) - 1)
    def _():
        o_ref[...] = x_ref[...] + 0.5 * _rms(acc_ref[...], post_g_ref[...])


def _ffn(x, pre_g, w_gu, w_down, post_g, tm):
    n = x.shape[0]
    nf = D_FF // FFN_TF
    return pl.pallas_call(
        _ffn_kernel,
        out_shape=jax.ShapeDtypeStruct((n, D_MODEL), F32),
        grid=(n // tm, nf),
        in_specs=[
            pl.BlockSpec((tm, D_MODEL), lambda i, f: (i, 0)),
            pl.BlockSpec((1, D_MODEL), lambda i, f: (0, 0)),
            pl.BlockSpec((D_MODEL, FFN_TF), lambda i, f: (0, f)),
            pl.BlockSpec((D_MODEL, FFN_TF), lambda i, f: (0, f + D_FF // FFN_TF)),
            pl.BlockSpec((FFN_TF, D_MODEL), lambda i, f: (f, 0)),
            pl.BlockSpec((1, D_MODEL), lambda i, f: (0, 0)),
        ],
        out_specs=pl.BlockSpec((tm, D_MODEL), lambda i, f: (i, 0)),
        scratch_shapes=[pltpu.VMEM((tm, D_MODEL), BF16), pltpu.VMEM((tm, D_MODEL), F32)],
        compiler_params=_cparams("parallel", "arbitrary"),
        name="ffn_half",
    )(x, pre_g, w_gu, w_gu, w_down, post_g)


def _absorb_kernel(wq_ref, wk_ref, o_ref):
    for h in range(MLA_HEADS):
        o_ref[:, h * LANES:(h + 1) * LANES] = _dot(wq_ref[h], wk_ref[h]).astype(BF16)


def _absorb(wq_nope, wk_t):
    return pl.pallas_call(
        _absorb_kernel,
        out_shape=jax.ShapeDtypeStruct((MLA_Q_LORA, MLA_HEADS * MLA_KV_LORA), BF16),
        name="absorb_uk",
    )(wq_nope, wk_t)


def _bias_from_dist(dist, table_ref, head):
    val = jnp.full(dist.shape, table_ref[0, head] * LOG2E, F32)
    for k in range(1, REL_BUCKETS):
        val = jnp.where(dist >= _BUCKET_THR[k], table_ref[k, head] * LOG2E, val)
    return jnp.where(dist < 0, MASK_VALUE, val)


def _bias_kernel(table_ref, bp_ref, bd_ref, bn_ref, *, past_len, dec_t):
    r = lax.broadcasted_iota(jnp.int32, (PROMPT_TQ, PROMPT_TK), 0)
    c = lax.broadcasted_iota(jnp.int32, (PROMPT_TQ, PROMPT_TK), 1)
    for head in range(DIFF_HEADS):
        for kind in range(2):
            bp_ref[head, kind] = _bias_from_dist(r - c + kind * PROMPT_TK, table_ref, head)
    t = lax.broadcasted_iota(jnp.int32, (dec_t, DEC_TK), 0)
    c = lax.broadcasted_iota(jnp.int32, (dec_t, DEC_TK), 1)
    tn = lax.broadcasted_iota(jnp.int32, (dec_t, LANES), 0)
    cn = lax.broadcasted_iota(jnp.int32, (dec_t, LANES), 1)
    for kvh in range(DIFF_KV_HEADS):
        for g in range(DIFF_GROUP):
            head = kvh * DIFF_GROUP + g
            far = _bias_from_dist(t - c + past_len, table_ref, head)
            last = _bias_from_dist(t - c + DEC_TK, table_ref, head)
            new = jnp.where(cn < dec_t, _bias_from_dist(tn - cn, table_ref, head), MASK_VALUE)
            for m in range(2):
                row = (g * 2 + m) * dec_t
                bd_ref[0, kvh, row:row + dec_t, :] = far
                bd_ref[1, kvh, row:row + dec_t, :] = last
                bn_ref[kvh, row:row + dec_t, :] = new


def _rel_bias_tiles(rel_table, past_len, dec_t):
    assert past_len - DEC_TK >= REL_FAR and dec_t <= LANES
    return pl.pallas_call(
        functools.partial(_bias_kernel, past_len=past_len, dec_t=dec_t),
        out_shape=(
            jax.ShapeDtypeStruct((DIFF_HEADS, 2, PROMPT_TQ, PROMPT_TK), F32),
            jax.ShapeDtypeStruct((2, DIFF_KV_HEADS, 4 * dec_t, DEC_TK), F32),
            jax.ShapeDtypeStruct((DIFF_KV_HEADS, 4 * dec_t, LANES), F32),
        ),
        in_specs=[pl.BlockSpec(memory_space=pltpu.SMEM)],
        name="rel_bias_tiles",
    )(rel_table)


def _mix_in_kernel(x_ref, pre_g_ref, w_in_ref, qg_ref, w_abs_ref, w_ra_ref, w_rb_ref, kvg_ref,
                   cos_ref, sin_ref,
                   q_ref, lat_t_ref, vx_ref, rows_ref, qd_ref, kd_t_ref, vdx_ref, kv_ref):
    tm = x_ref.shape[0]
    tk = PROMPT_TK
    h = _rms(x_ref[...], pre_g_ref[...]).astype(BF16)
    p = _dot(h, w_in_ref[...])
    cq = p[:, 0:256]
    ckv = p[:, 256:384]
    kr_a = p[:, 384:512]
    kr_b = p[:, 512:640]
    dq = p[:, 640:1152]
    dk = p[:, 1152:1408]
    dv = p[:, 1408:1664]
    cos = cos_ref[...]
    sin = sin_ref[...]

    cqn = _rms(cq, qg_ref[...]).astype(BF16)
    q_lat = _dot(cqn, w_abs_ref[...])
    r_a = _dot(cqn, w_ra_ref[...])
    r_b = _dot(cqn, w_rb_ref[...])
    for hd in range(MLA_HEADS):
        sl = slice(hd * LANES, (hd + 1) * LANES)
        q_ref[hd, :, 0:LANES] = (q_lat[:, sl] * MLA_QSCALE).astype(q_ref.dtype)
        q_ref[hd, :, LANES:QPAD] = ((r_a[:, sl] * cos + r_b[:, sl] * sin) * MLA_QSCALE).astype(q_ref.dtype)

    ckvn = _rms(ckv, kvg_ref[...])
    k_rope = kr_a * cos + kr_b * sin
    rows_ref[:, 0:MLA_KV_LORA] = ckvn
    rows_ref[:, MLA_KV_LORA:MLA_LAT] = k_rope[:, 0:MLA_ROPE]
    for c in range(tm // tk):
        rs = slice(c * tk, (c + 1) * tk)
        lat_t_ref[c, 0:LANES, :] = ckvn[rs].T.astype(BF16)
        lat_t_ref[c, LANES:QPAD, :] = k_rope[rs].T.astype(BF16)
        for kvh in range(DIFF_KV_HEADS):
            kd_t_ref[kvh, c] = dk[rs, kvh * LANES:(kvh + 1) * LANES].T.astype(BF16)
    ones = jnp.ones((tm, LANES), BF16)
    vx_ref[:, 0:LANES] = ckvn.astype(BF16)
    vx_ref[:, LANES:QPAD] = ones
    for kvh in range(DIFF_KV_HEADS):
        vdx_ref[:, 2 * kvh * LANES:(2 * kvh + 1) * LANES] = dv[:, kvh * LANES:(kvh + 1) * LANES].astype(BF16)
        vdx_ref[:, (2 * kvh + 1) * LANES:(2 * kvh + 2) * LANES] = ones

    lane = lax.broadcasted_iota(jnp.int32, (tm, LANES), 1)
    for kvh in range(DIFF_KV_HEADS):
        for g in range(DIFF_GROUP):
            pair = dq[:, (kvh * 2 + g) * LANES:(kvh * 2 + g + 1) * LANES] * DIFF_QSCALE
            for m in range(2):
                keep = (lane < DIFF_DH) if m == 0 else (lane >= DIFF_DH)
                qd_ref[kvh * 4 + g * 2 + m] = jnp.where(keep, pair, 0.0).astype(qd_ref.dtype)
    for j, piece in enumerate((dk[:, 0:LANES], dk[:, LANES:2 * LANES], dv[:, 0:LANES], dv[:, LANES:2 * LANES])):
        kv_ref[pl.ds(j, tm, stride=4), :] = piece


def _mix_in(x, wl, cos_t, sin_t, tm, q_dtype):
    n = x.shape[0]
    n_tab = cos_t.shape[0] // tm
    tk = PROMPT_TK
    const = lambda i: (0, 0)
    w = lambda a: pl.BlockSpec(a.shape, const)
    return pl.pallas_call(
        _mix_in_kernel,
        out_shape=(
            jax.ShapeDtypeStruct((MLA_HEADS, n, QPAD), q_dtype),
            jax.ShapeDtypeStruct((n // tk, QPAD, tk), BF16),
            jax.ShapeDtypeStruct((n, QPAD), BF16),
            jax.ShapeDtypeStruct((n, MLA_LAT), F32),
            jax.ShapeDtypeStruct((8, n, LANES), q_dtype),
            jax.ShapeDtypeStruct((DIFF_KV_HEADS, n // tk, LANES, tk), BF16),
            jax.ShapeDtypeStruct((n, 2 * DIFF_KV_HEADS * LANES), BF16),
            jax.ShapeDtypeStruct((4 * n, LANES), F32),
        ),
        grid=(n // tm,),
        in_specs=[
            pl.BlockSpec((tm, D_MODEL), lambda i: (i, 0)),
            w(wl['mix_pre_g']), w(wl['w_in_a']), w(wl['q_norm_g']), w(wl['w_abs']),
            w(wl['w_rope_a']), w(wl['w_rope_b']), w(wl['kv_norm_g']),
            pl.BlockSpec((tm, LANES), lambda i: (i % n_tab, 0)),
            pl.BlockSpec((tm, LANES), lambda i: (i % n_tab, 0)),
        ],
        out_specs=(
            pl.BlockSpec((MLA_HEADS, tm, QPAD), lambda i: (0, i, 0)),
            pl.BlockSpec((tm // tk, QPAD, tk), lambda i: (i, 0, 0)),
            pl.BlockSpec((tm, QPAD), lambda i: (i, 0)),
            pl.BlockSpec((tm, MLA_LAT), lambda i: (i, 0)),
            pl.BlockSpec((8, tm, LANES), lambda i: (0, i, 0)),
            pl.BlockSpec((DIFF_KV_HEADS, tm // tk, LANES, tk), lambda i: (0, i, 0, 0)),
            pl.BlockSpec((tm, 2 * DIFF_KV_HEADS * LANES), lambda i: (i, 0)),
            pl.BlockSpec((4 * tm, LANES), lambda i: (i, 0)),
        ),
        compiler_params=_cparams("parallel"),
        name="mix_in",
    )(x, wl['mix_pre_g'], wl['w_in_a'], wl['q_norm_g'], wl['w_abs'], wl['w_rope_a'], wl['w_rope_b'],
      wl['kv_norm_g'], cos_t, sin_t)


def _online_step(s, v, m, l, acc, shift=None, v_transposed=False):
    m_blk = jnp.max(s, axis=-1, keepdims=True)
    if shift is not None:
        m_blk = m_blk + shift
    m_new = jnp.maximum(m, m_blk)
    alpha = jnp.exp2(m - m_new)
    p = jnp.exp2(s - (m_new if shift is None else m_new - shift))
    l_new = alpha * l + jnp.sum(p, axis=-1, keepdims=True)
    pv = _dot_nt(p.astype(BF16), v) if v_transposed else _dot(p.astype(BF16), v)
    acc_new = alpha * acc + pv
    return m_new, l_new, acc_new


def _flash_step(s, vx, m_ref, acc_ref, shift=None):
    tiles = [s[:, t * LANES:(t + 1) * LANES] for t in range(s.shape[1] // LANES)]
    m_blk = jnp.max(functools.reduce(jnp.maximum, tiles), axis=-1, keepdims=True)
    m_prev = m_ref[...]
    m_new = jnp.maximum(m_prev, m_blk if shift is None else m_blk + shift)
    alpha = jnp.exp2(m_prev - m_new)
    sub = m_new if shift is None else m_new - shift
    p = jnp.concatenate([jnp.exp2(t - sub) for t in tiles], axis=1).astype(BF16)
    acc_ref[...] = jnp.concatenate([alpha, alpha], axis=1) * acc_ref[...] + _dot(p, vx)
    m_ref[...] = m_new


def _flash_init(m_ref, acc_ref):
    m_ref[...] = jnp.full_like(m_ref, -jnp.inf)
    acc_ref[...] = jnp.zeros_like(acc_ref)


def _mla_prompt_kernel(q_ref, kt_ref, vx_ref, o_ref, m_ref, acc_ref):
    i = pl.program_id(1)
    tq, tk = PROMPT_TQ, PROMPT_TK
    rows = MLA_HEADS * tq
    q = q_ref[...].reshape(rows, QPAD)
    _flash_init(m_ref, acc_ref)

    def chunk(j, mask):
        s = _dot(q, kt_ref[j])
        if mask is not None:
            s = jnp.where(mask, s, MASK_VALUE)
        _flash_step(s, vx_ref[pl.ds(pl.multiple_of(j * tk, tk), tk), :], m_ref, acc_ref)

    def body(j, carry):
        chunk(j, None)
        return carry

    lax.fori_loop(0, i, body, 0)
    r = lax.broadcasted_iota(jnp.int32, (MLA_HEADS, tq, tk), 1).reshape(rows, tk)
    c = lax.broadcasted_iota(jnp.int32, (rows, tk), 1)
    chunk(i, c <= r)

    for hd in range(MLA_HEADS):
        a = acc_ref[hd * tq:(hd + 1) * tq, :]
        o_ref[:, hd * LANES:(hd + 1) * LANES] = (a[:, 0:LANES] * (1.0 / a[:, LANES:QPAD])).astype(o_ref.dtype)


def _mla_prompt(q, lat_t, vx, batch, seq):
    tq, tk = PROMPT_TQ, PROMPT_TK
    assert tq == tk
    nq = seq // tq
    rows = MLA_HEADS * tq
    return pl.pallas_call(
        _mla_prompt_kernel,
        out_shape=jax.ShapeDtypeStruct((batch * seq, MLA_HEADS * MLA_KV_LORA), BF16),
        grid=(batch, nq),
        in_specs=[
            pl.BlockSpec((MLA_HEADS, tq, QPAD), lambda b, i: (0, b * nq + i, 0)),
            pl.BlockSpec((seq // tk, QPAD, tk), lambda b, i: (b, 0, 0)),
            pl.BlockSpec((seq, QPAD), lambda b, i: (b, 0)),
        ],
        out_specs=pl.BlockSpec((tq, MLA_HEADS * MLA_KV_LORA), lambda b, i: (b * nq + i, 0)),
        scratch_shapes=[pltpu.VMEM((rows, LANES), F32), pltpu.VMEM((rows, QPAD), F32)],
        compiler_params=_cparams("parallel", "arbitrary"),
        name="mla_prompt",
    )(q, lat_t, vx)


def _diff_lambda(lq1, lk1, lq2, lk2, lam_init):
    return (jnp.exp(jnp.sum(lq1 * lk1, axis=-1, keepdims=True))
            - jnp.exp(jnp.sum(lq2 * lk2, axis=-1, keepdims=True)) + lam_init)


def _diff_combine(o0, o1, lam, subln_g, lam_init):
    return _rms(o0 - lam * o1, subln_g) * (1.0 - lam_init)


def _diff_prompt_kernel(q_ref, kt_ref, vx_ref, bias_ref, far_ref, lq1_ref, lk1_ref, lq2_ref, lk2_ref,
                        subln_ref, o_ref, m_ref, acc_ref, *, lam_init):
    i = pl.program_id(2)
    tq, tk = PROMPT_TQ, PROMPT_TK
    rows = 4 * tq
    q = q_ref[...].reshape(rows, LANES)
    _flash_init(m_ref, acc_ref)

    def chunk(j, kind):
        s = _dot(q, kt_ref[j])
        shift = far_ref[...]
        if kind is not None:
            s = (s.reshape(DIFF_GROUP, 2, tq, tk) + bias_ref[:, kind][:, None]).reshape(rows, tk)
            shift = None
        _flash_step(s, vx_ref[pl.ds(pl.multiple_of(j * tk, tk), tk), :], m_ref, acc_ref, shift)

    def pair(jj, carry):
        s = jnp.concatenate([_dot(q, kt_ref[2 * jj]), _dot(q, kt_ref[2 * jj + 1])], axis=1)
        _flash_step(s, vx_ref[pl.ds(pl.multiple_of(jj * 2 * tk, 2 * tk), 2 * tk), :], m_ref, acc_ref,
                    far_ref[...])
        return carry

    n_far = jnp.maximum(i - 1, 0)
    lax.fori_loop(0, n_far // 2, pair, 0)

    @pl.when(n_far % 2 == 1)
    def _():
        chunk(n_far - 1, None)

    @pl.when(i > 0)
    def _():
        chunk(i - 1, 1)

    chunk(i, 0)

    lam = _diff_lambda(lq1_ref[...], lk1_ref[...], lq2_ref[...], lk2_ref[...], lam_init)
    acc = acc_ref[...]
    o = acc[:, 0:LANES] * (1.0 / acc[:, LANES:2 * LANES])
    for g in range(DIFF_GROUP):
        o0 = o[(2 * g) * tq:(2 * g + 1) * tq, :]
        o1 = o[(2 * g + 1) * tq:(2 * g + 2) * tq, :]
        o_ref[:, g * LANES:(g + 1) * LANES] = _diff_combine(
            o0, o1, lam, subln_ref[...], lam_init).astype(o_ref.dtype)


def _diff_prompt(qd, kd_t, vdx, bias_p, far_rows, wl, lam_init, batch, seq):
    tq, tk = PROMPT_TQ, PROMPT_TK
    nq = seq // tq
    rows = 4 * tq
    assert tq == tk and tk >= REL_FAR
    const3 = lambda b, h, i: (0, 0)
    small = lambda a: pl.BlockSpec(a.shape, const3)
    return pl.pallas_call(
        functools.partial(_diff_prompt_kernel, lam_init=lam_init),
        out_shape=jax.ShapeDtypeStruct((batch * seq, DIFF_HEADS * 2 * DIFF_DH), BF16),
        grid=(batch, DIFF_KV_HEADS, nq),
        in_specs=[
            pl.BlockSpec((4, tq, LANES), lambda b, h, i: (h, b * nq + i, 0)),
            pl.BlockSpec((None, seq // tk, LANES, tk), lambda b, h, i: (h, b, 0, 0)),
            pl.BlockSpec((seq, 2 * LANES), lambda b, h, i: (b, h)),
            pl.BlockSpec((DIFF_GROUP, 2, tq, tk), lambda b, h, i: (h, 0, 0, 0)),
            pl.BlockSpec((None, rows, LANES), lambda b, h, i: (h, 0, 0)),
            small(wl['lq1']), small(wl['lk1']), small(wl['lq2']), small(wl['lk2']), small(wl['subln_g']),
        ],
        out_specs=pl.BlockSpec((tq, DIFF_GROUP * LANES), lambda b, h, i: (b * nq + i, h)),
        scratch_shapes=[pltpu.VMEM((rows, LANES), F32), pltpu.VMEM((rows, 2 * LANES), F32)],
        compiler_params=_cparams("parallel", "parallel", "arbitrary"),
        name="diff_prompt",
    )(qd, kd_t, vdx, bias_p, far_rows, wl['lq1'], wl['lk1'], wl['lq2'], wl['lk2'], wl['subln_g'])


def _decode_kernel(pt_ref, qm_ref, qd_ref, rows_ref, kv_ref, bias_ref, biasn_ref,
                   lq1_ref, lk1_ref, lq2_ref, lk2_ref, subln_ref, cm_hbm, cd_hbm,
                   om_ref, od_ref, mbuf, dbuf, nlat, nkv, sem,
                   *, lam_init, layer, n_chunks, dec_t):
    b = pl.program_id(0)
    nb = pl.num_programs(0)
    mrows = MLA_HEADS * dec_t
    drows = 4 * dec_t
    prow = PAGE_SIZE * 4

    def copies(bb, c, slot):
        out = []
        for p in range(DEC_PAGES):
            phys = pt_ref[bb, c * DEC_PAGES + p]
            out.append(pltpu.make_async_copy(cm_hbm.at[layer, phys],
                                             mbuf.at[slot, :, pl.ds(p * PAGE_SIZE, PAGE_SIZE)], sem.at[0, slot]))
            out.append(pltpu.make_async_copy(cd_hbm.at[layer, phys], dbuf.at[slot, pl.ds(p * prow, prow)],
                                             sem.at[1, slot]))
        return out

    def start(bb, c, slot):
        for n, cp in enumerate(copies(bb, c, slot)):
            cp.start(priority=(n // 2) % 2)

    @pl.when(b == 0)
    def _():
        nlat[...] = jnp.zeros_like(nlat)
        nkv[...] = jnp.zeros_like(nkv)
        start(0, 0, 0)

    qm = qm_ref[...].reshape(mrows, QPAD).astype(BF16)
    qd = [qd_ref[kvh * 4:(kvh + 1) * 4].reshape(drows, LANES).astype(BF16) for kvh in range(DIFF_KV_HEADS)]

    def init(r, d):
        return (jnp.full((r, 1), -jnp.inf, F32), jnp.zeros((r, 1), F32), jnp.zeros((r, d), F32))

    def body(c, carry):
        st_m, st_d = carry
        slot = c % 2

        @pl.when(c + 1 < n_chunks)
        def _():
            start(b, c + 1, 1 - slot)

        @pl.when(jnp.logical_and(c + 1 == n_chunks, b + 1 < nb))
        def _():
            start(b + 1, 0, 1 - slot)

        for cp in copies(b, c, slot):
            cp.wait()

        kt = mbuf[slot].astype(BF16)
        st_m = _online_step(_dot(qm[:, 0:MLA_LAT], kt), kt[0:MLA_KV_LORA, :], *st_m, v_transposed=True)

        kind = lax.convert_element_type(c == n_chunks - 1, jnp.int32)
        new_d = []
        for kvh in range(DIFF_KV_HEADS):
            kd = dbuf[slot, pl.ds(kvh, DEC_TK, stride=4), :].astype(BF16)
            vd = dbuf[slot, pl.ds(2 + kvh, DEC_TK, stride=4), :].astype(BF16)
            s = _dot_nt(qd[kvh], kd) + bias_ref[kind, kvh]
            new_d.append(_online_step(s, vd, *st_d[kvh]))
        return st_m, tuple(new_d)

    st_m, st_d = lax.fori_loop(
        0, n_chunks, body,
        (init(mrows, MLA_KV_LORA), tuple(init(drows, LANES) for _ in range(DIFF_KV_HEADS))))

    nlat[0:dec_t, 0:MLA_KV_LORA] = rows_ref[:, 0:MLA_KV_LORA]
    nlat[0:dec_t, MLA_KV_LORA:MLA_LAT] = rows_ref[:, MLA_KV_LORA:MLA_LAT]
    for j in range(4):
        nkv[0:dec_t, j * LANES:(j + 1) * LANES] = kv_ref[pl.ds(j, dec_t, stride=4), :]
    new_k = nlat[...].astype(BF16)
    rr = lax.broadcasted_iota(jnp.int32, (MLA_HEADS, dec_t, LANES), 1).reshape(mrows, LANES)
    cc = lax.broadcasted_iota(jnp.int32, (mrows, LANES), 1)
    s = jnp.where(cc <= rr, _dot_nt(qm, new_k), MASK_VALUE)
    _, l_m, acc_m = _online_step(s, new_k[:, 0:MLA_KV_LORA], *st_m)
    o_m = acc_m * (1.0 / l_m)
    for hd in range(MLA_HEADS):
        om_ref[:, hd * LANES:(hd + 1) * LANES] = o_m[hd * dec_t:(hd + 1) * dec_t, :]

    lam = _diff_lambda(lq1_ref[...], lk1_ref[...], lq2_ref[...], lk2_ref[...], lam_init)
    for kvh in range(DIFF_KV_HEADS):
        kd = nkv[:, kvh * LANES:(kvh + 1) * LANES].astype(BF16)
        vd = nkv[:, (2 + kvh) * LANES:(3 + kvh) * LANES].astype(BF16)
        s = _dot_nt(qd[kvh], kd) + biasn_ref[kvh]
        _, l_d, acc_d = _online_step(s, vd, *st_d[kvh])
        o = acc_d * (1.0 / l_d)
        for g in range(DIFF_GROUP):
            o0 = o[(2 * g) * dec_t:(2 * g + 1) * dec_t, :]
            o1 = o[(2 * g + 1) * dec_t:(2 * g + 2) * dec_t, :]
            od_ref[:, (kvh * 2 + g) * LANES:(kvh * 2 + g + 1) * LANES] = _diff_combine(
                o0, o1, lam, subln_ref[...], lam_init)


def _decode(page_table, qm, qd, rows, kv, bias_d, bias_n, wl, cache_mla, cache_diff,
            lam_init, layer, dec_b, dec_t):
    n_pages = page_table.shape[1]
    assert n_pages % DEC_PAGES == 0 and (n_pages // DEC_PAGES) % 2 == 0
    n_chunks = n_pages // DEC_PAGES
    n_pool = cache_diff.shape[1]
    cd = cache_diff.reshape(cache_diff.shape[0], n_pool, PAGE_SIZE * 4, LANES)
    cm = jnp.swapaxes(cache_mla, 2, 3)
    const = lambda b, pt: (0, 0)
    small = lambda a: pl.BlockSpec(a.shape, const)
    grid_spec = pltpu.PrefetchScalarGridSpec(
        num_scalar_prefetch=1,
        grid=(dec_b,),
        in_specs=[
            pl.BlockSpec((MLA_HEADS, dec_t, QPAD), lambda b, pt: (0, b, 0)),
            pl.BlockSpec((8, dec_t, LANES), lambda b, pt: (0, b, 0)),
            pl.BlockSpec((dec_t, MLA_LAT), lambda b, pt: (b, 0)),
            pl.BlockSpec((4 * dec_t, LANES), lambda b, pt: (b, 0)),
            pl.BlockSpec(bias_d.shape, lambda b, pt: (0, 0, 0, 0)),
            pl.BlockSpec(bias_n.shape, lambda b, pt: (0, 0, 0)),
            small(wl['lq1']), small(wl['lk1']), small(wl['lq2']), small(wl['lk2']), small(wl['subln_g']),
            pl.BlockSpec(memory_space=pl.ANY),
            pl.BlockSpec(memory_space=pl.ANY),
        ],
        out_specs=(
            pl.BlockSpec((dec_t, MLA_HEADS * MLA_KV_LORA), lambda b, pt: (b, 0)),
            pl.BlockSpec((dec_t, DIFF_HEADS * LANES), lambda b, pt: (b, 0)),
        ),
        scratch_shapes=[
            pltpu.VMEM((2, MLA_LAT, DEC_TK), F32),
            pltpu.VMEM((2, DEC_TK * 4, LANES), F32),
            pltpu.VMEM((LANES, QPAD), F32),
            pltpu.VMEM((LANES, 4 * LANES), F32),
            pltpu.SemaphoreType.DMA((2, 2)),
        ],
    )
    return pl.pallas_call(
        functools.partial(_decode_kernel, lam_init=lam_init, layer=layer, n_chunks=n_chunks, dec_t=dec_t),
        out_shape=(
            jax.ShapeDtypeStruct((dec_b * dec_t, MLA_HEADS * MLA_KV_LORA), F32),
            jax.ShapeDtypeStruct((dec_b * dec_t, DIFF_HEADS * LANES), F32),
        ),
        grid_spec=grid_spec,
        compiler_params=_cparams("arbitrary"),
        name="paged_decode",
    )(page_table, qm, qd, rows, kv, bias_d, bias_n,
      wl['lq1'], wl['lk1'], wl['lq2'], wl['lk2'], wl['subln_g'], cm, cd)


def _merge_kernel(x_ref, om_ref, od_ref, pre_g_ref, wg_ref, wuv_ref, woa_ref, wob_ref, wout_ref,
                  post_g_ref, o_ref):
    x = x_ref[...]
    h = _rms(x, pre_g_ref[...]).astype(BF16)
    gates = _dot(h, wg_ref[...])
    v_a = _dot(om_ref[...].astype(BF16), wuv_ref[...])
    y_a = _dot(v_a.astype(BF16), woa_ref[...])
    y_b = _dot(od_ref[...].astype(BF16), wob_ref[...])
    z = jax.nn.sigmoid(gates[:, 0:D_MODEL]) * y_a + jax.nn.sigmoid(gates[:, D_MODEL:2 * D_MODEL]) * y_b
    y = _dot(z.astype(BF16), wout_ref[...])
    o_ref[...] = x + _rms(y, post_g_ref[...])


def _merge(x, o_m, o_d, wl, tm):
    n = x.shape[0]
    const = lambda i: (0, 0)
    w = lambda a: pl.BlockSpec(a.shape, const)
    row = lambda width: pl.BlockSpec((tm, width), lambda i: (i, 0))
    return pl.pallas_call(
        _merge_kernel,
        out_shape=jax.ShapeDtypeStruct((n, D_MODEL), F32),
        grid=(n // tm,),
        in_specs=[row(D_MODEL), row(o_m.shape[1]), row(o_d.shape[1]),
                  w(wl['mix_pre_g']), w(wl['w_gates']), w(wl['w_uv_bd']), w(wl['w_o_mla']),
                  w(wl['w_o_diff']), w(wl['w_out']), w(wl['mix_post_g'])],
        out_specs=row(D_MODEL),
        compiler_params=_cparams("parallel"),
        name="merge",
    )(x, o_m, o_d, wl['mix_pre_g'], wl['w_gates'], wl['w_uv_bd'], wl['w_o_mla'], wl['w_o_diff'],
      wl['w_out'], wl['mix_post_g'])


def _rope_tables(pos, n_rows):
    half = MLA_ROPE // 2
    inv_freq = ROPE_THETA ** (-jnp.arange(half, dtype=F32) / half)
    ang = pos.astype(F32)[:, None] * inv_freq[None, :]
    cos, sin = jnp.cos(ang), jnp.sin(ang)
    zeros = jnp.zeros((pos.shape[0], LANES - MLA_ROPE), F32)
    cos_t = jnp.concatenate([cos, cos, zeros], axis=1)
    sin_t = jnp.concatenate([-sin, sin, zeros], axis=1)
    reps = n_rows // pos.shape[0]
    return jnp.tile(cos_t, (reps, 1)), jnp.tile(sin_t, (reps, 1))


def _layer_weights(layer, p):
    g = lambda name: p[name][layer].reshape(1, -1)
    w_in = p['w_in'][layer]
    o = np.cumsum((MLA_Q_LORA, MLA_KV_LORA, MLA_ROPE, DIFF_HEADS * 2 * DIFF_DH,
                   DIFF_KV_HEADS * 2 * DIFF_DH, DIFF_KV_HEADS * 2 * DIFF_DH)).tolist()
    half = MLA_ROPE // 2
    kr = w_in[:, o[1]:o[2]]
    zpad = jnp.zeros((D_MODEL, LANES - MLA_ROPE), F32)
    kr_a = jnp.concatenate([kr, zpad], axis=1)
    kr_b = jnp.concatenate([kr[:, half:], kr[:, :half], zpad], axis=1)
    w_in_a = jnp.concatenate([w_in[:, :o[1]], kr_a, kr_b, w_in[:, o[2]:o[5]]], axis=1).astype(BF16)

    w_uq = p['w_uq'][layer].reshape(MLA_Q_LORA, MLA_HEADS, MLA_NOPE + MLA_ROPE)
    wq_nope = jnp.moveaxis(w_uq[:, :, :MLA_NOPE], 1, 0).astype(BF16)
    wk_t = jnp.transpose(p['w_uk'][layer], (1, 2, 0)).astype(BF16)
    x1 = w_uq[:, :, MLA_NOPE:MLA_NOPE + half]
    x2 = w_uq[:, :, MLA_NOPE + half:]
    hz = jnp.zeros((MLA_Q_LORA, MLA_HEADS, LANES - MLA_ROPE), F32)
    w_rope_a = jnp.concatenate([x1, x2, hz], axis=2).reshape(MLA_Q_LORA, MLA_HEADS * LANES).astype(BF16)
    w_rope_b = jnp.concatenate([x2, x1, hz], axis=2).reshape(MLA_Q_LORA, MLA_HEADS * LANES).astype(BF16)

    eye = jnp.eye(MLA_HEADS, dtype=F32)
    w_uv_bd = jnp.einsum('lhv,hg->hlgv', p['w_uv'][layer], eye).reshape(
        MLA_HEADS * MLA_KV_LORA, MLA_HEADS * MLA_V).astype(BF16)
    return {
        'ffn1_pre_g': g('ffn1_pre_g'), 'ffn1_w_gu': p['ffn1_w_gu'][layer].astype(BF16),
        'ffn1_w_down': p['ffn1_w_down'][layer].astype(BF16), 'ffn1_post_g': g('ffn1_post_g'),
        'ffn2_pre_g': g('ffn2_pre_g'), 'ffn2_w_gu': p['ffn2_w_gu'][layer].astype(BF16),
        'ffn2_w_down': p['ffn2_w_down'][layer].astype(BF16), 'ffn2_post_g': g('ffn2_post_g'),
        'mix_pre_g': g('mix_pre_g'), 'w_in_a': w_in_a, 'w_gates': w_in[:, o[5]:].astype(BF16),
        'q_norm_g': g('mla_q_norm_g'), 'kv_norm_g': g('mla_kv_norm_g'),
        'w_abs': _absorb(wq_nope, wk_t), 'w_rope_a': w_rope_a, 'w_rope_b': w_rope_b,
        'w_uv_bd': w_uv_bd, 'w_o_mla': p['w_o_mla'][layer].astype(BF16),
        'w_o_diff': p['w_o_diff'][layer].astype(BF16), 'w_out': p['w_out'][layer].astype(BF16),
        'mix_post_g': g('mix_post_g'),
        'lq1': g('diff_lq1'), 'lk1': g('diff_lk1'), 'lq2': g('diff_lq2'), 'lk2': g('diff_lk2'),
        'subln_g': g('diff_subln_g'),
    }


def _far_rows(rel_table, rows_per_map):
    far = (rel_table[REL_BUCKETS - 1].astype(F32) * LOG2E).reshape(DIFF_KV_HEADS, DIFF_GROUP, 1, 1)
    return jnp.broadcast_to(far, (DIFF_KV_HEADS, DIFF_GROUP, 2 * rows_per_map, LANES)).reshape(
        DIFF_KV_HEADS, 4 * rows_per_map, LANES)


def kernel(x_prompt, x_sample, cache_mla, cache_diff, page_table, ffn1_pre_g, ffn1_w_gu, ffn1_w_down, ffn1_post_g, mix_pre_g, w_in, mla_q_norm_g, w_uq, mla_kv_norm_g, w_uk, w_uv, diff_lq1, diff_lk1, diff_lq2, diff_lk2, diff_subln_g, rel_table, w_o_mla, w_o_diff, w_out, mix_post_g, ffn2_pre_g, ffn2_w_gu, ffn2_w_down, ffn2_post_g):
    params = dict(
        ffn1_pre_g=ffn1_pre_g, ffn1_w_gu=ffn1_w_gu, ffn1_w_down=ffn1_w_down, ffn1_post_g=ffn1_post_g,
        mix_pre_g=mix_pre_g, w_in=w_in, mla_q_norm_g=mla_q_norm_g, w_uq=w_uq,
        mla_kv_norm_g=mla_kv_norm_g, w_uk=w_uk, w_uv=w_uv, diff_lq1=diff_lq1, diff_lk1=diff_lk1,
        diff_lq2=diff_lq2, diff_lk2=diff_lk2, diff_subln_g=diff_subln_g, w_o_mla=w_o_mla,
        w_o_diff=w_o_diff, w_out=w_out, mix_post_g=mix_post_g, ffn2_pre_g=ffn2_pre_g,
        ffn2_w_gu=ffn2_w_gu, ffn2_w_down=ffn2_w_down, ffn2_post_g=ffn2_post_g)
    batch, seq, _ = x_prompt.shape
    dec_b, dec_t, _ = x_sample.shape
    depth = cache_mla.shape[0]
    past_len = page_table.shape[1] * PAGE_SIZE
    n_p, n_s = batch * seq, dec_b * dec_t
    tm_p, tm_s = 512, 512

    cos_p, sin_p = _rope_tables(jnp.arange(seq), seq)
    cos_s, sin_s = _rope_tables(past_len + jnp.arange(dec_t), tm_s)
    bias_p, bias_d, bias_n = _rel_bias_tiles(rel_table.astype(F32), past_len, dec_t)
    far_p = _far_rows(rel_table, PROMPT_TQ)

    y_p = x_prompt.reshape(n_p, D_MODEL)
    y_s = x_sample.reshape(n_s, D_MODEL)
    outs = ([], [], [], [])
    for layer in range(depth):
        wl = _layer_weights(layer, params)
        lam_init = 0.8 - 0.6 * math.exp(-0.3 * layer)

        y_p = _ffn(y_p, wl['ffn1_pre_g'], wl['ffn1_w_gu'], wl['ffn1_w_down'], wl['ffn1_post_g'], 1024)
        q, lat_t, vx, rows_p, qd, kd_t, vdx, kv_p = _mix_in(y_p, wl, cos_p, sin_p, tm_p, BF16)
        o_m = _mla_prompt(q, lat_t, vx, batch, seq)
        o_d = _diff_prompt(qd, kd_t, vdx, bias_p, far_p, wl, lam_init, batch, seq)
        y_p = _merge(y_p, o_m, o_d, wl, tm_p)
        y_p = _ffn(y_p, wl['ffn2_pre_g'], wl['ffn2_w_gu'], wl['ffn2_w_down'], wl['ffn2_post_g'], 1024)

        y_s = _ffn(y_s, wl['ffn1_pre_g'], wl['ffn1_w_gu'], wl['ffn1_w_down'], wl['ffn1_post_g'], tm_s)
        q, _, _, rows_s, qd, _, _, kv_s = _mix_in(y_s, wl, cos_s, sin_s, tm_s, F32)
        o_m, o_d = _decode(page_table, q, qd, rows_s, kv_s, bias_d, bias_n, wl,
                           cache_mla, cache_diff, lam_init, layer, dec_b, dec_t)
        y_s = _merge(y_s, o_m, o_d, wl, tm_s)
        y_s = _ffn(y_s, wl['ffn2_pre_g'], wl['ffn2_w_gu'], wl['ffn2_w_down'], wl['ffn2_post_g'], tm_s)

        outs[0].append(rows_p.reshape(batch, seq, MLA_LAT))
        outs[1].append(kv_p.reshape(batch, seq, 2, DIFF_KV_HEADS, 2 * DIFF_DH))
        outs[2].append(rows_s.reshape(dec_b, dec_t, MLA_LAT))
        outs[3].append(kv_s.reshape(dec_b, dec_t, 2, DIFF_KV_HEADS, 2 * DIFF_DH))
    return (y_p.reshape(batch, seq, D_MODEL), y_s.reshape(dec_b, dec_t, D_MODEL),
            jnp.stack(outs[0]), jnp.stack(outs[1]), jnp.stack(outs[2]), jnp.stack(outs[3]))
```

```python
import functools
import math

import numpy as np
import jax
import jax.numpy as jnp
from jax import lax
from jax.experimental import pallas as pl
from jax.experimental.pallas import tpu as pltpu

F32 = jnp.float32
BF16 = jnp.bfloat16

D_MODEL = 1024
MLA_HEADS = 8
MLA_NOPE = 64
MLA_ROPE = 32
MLA_V = 64
MLA_Q_LORA = D_MODEL // 4
MLA_KV_LORA = D_MODEL // 8
MLA_LAT = MLA_KV_LORA + MLA_ROPE
MLA_SCALE = (MLA_NOPE + MLA_ROPE) ** -0.5
DIFF_HEADS = 4
DIFF_KV_HEADS = 2
DIFF_GROUP = DIFF_HEADS // DIFF_KV_HEADS
DIFF_DH = 64
DIFF_SCALE = DIFF_DH ** -0.5
REL_BUCKETS = 32
REL_MAX_DIST = 128
D_FF = ((8 * D_MODEL // 3 + 127) // 128) * 128
ROPE_THETA = 10000.0
NORM_EPS = 1e-6
MASK_VALUE = -1e30
PAGE_SIZE = 128
LOG2E = math.log2(math.e)
MLA_QSCALE = MLA_SCALE * LOG2E
DIFF_QSCALE = DIFF_SCALE * LOG2E

LANES = 128
QPAD = 2 * LANES
VMEM_LIMIT = 56 * 1024 * 1024

FFN_TF = 256
PROMPT_TQ = 256
PROMPT_TK = 256
DEC_PAGES = 32
DEC_TK = DEC_PAGES * PAGE_SIZE
DEC_SLOTS = 3


def _bucket_thresholds():
    max_exact = REL_BUCKETS // 2
    n = np.arange(0, 4 * REL_MAX_DIST)
    large = max_exact + (np.log(np.maximum(n, 1).astype(np.float32) / max_exact)
                         / math.log(REL_MAX_DIST / max_exact)
                         * (REL_BUCKETS - max_exact)).astype(np.int32)
    bucket = np.where(n < max_exact, n, np.minimum(large, REL_BUCKETS - 1))
    assert np.all(np.diff(bucket) >= 0)
    return [int(np.argmax(bucket >= k)) for k in range(REL_BUCKETS)]


_BUCKET_THR = _bucket_thresholds()
REL_FAR = _BUCKET_THR[REL_BUCKETS - 1]


def _rms(x, g):
    return x * lax.rsqrt(jnp.mean(x * x, axis=-1, keepdims=True) + NORM_EPS) * g


def _dot(a, b):
    return jnp.dot(a, b, preferred_element_type=F32)


def _dot_nt(a, b):
    return lax.dot_general(a, b, (((1,), (1,)), ((), ())), preferred_element_type=F32)


def _cparams(*sem):
    return pltpu.CompilerParams(dimension_semantics=sem, vmem_limit_bytes=VMEM_LIMIT)


def _ffn_kernel(x_ref, pre_g_ref, wg_ref, wu_ref, wd_ref, post_g_ref, o_ref, h_ref, acc_ref):
    f = pl.program_id(1)

    @pl.when(f == 0)
    def _():
        h_ref[...] = _rms(x_ref[...], pre_g_ref[...]).astype(BF16)
        acc_ref[...] = jnp.zeros_like(acc_ref)

    h = h_ref[...]
    gate = _dot(h, wg_ref[...])
    up = _dot(h, wu_ref[...])
    act = (gate * jax.nn.sigmoid(gate) * up).astype(BF16)
    acc_ref[...] += _dot(act, wd_ref[...])

    @pl.when(f == pl.num_programs(1) - 1)
    def _():
        o_ref[...] = x_ref[...] + 0.5 * _rms(acc_ref[...], post_g_ref[...])


def _ffn(x, pre_g, w_gu, w_down, post_g, tm):
    n = x.shape[0]
    nf = D_FF // FFN_TF
    return pl.pallas_call(
        _ffn_kernel,
        out_shape=jax.ShapeDtypeStruct((n, D_MODEL), F32),
        grid=(n // tm, nf),
        in_specs=[
            pl.BlockSpec((tm, D_MODEL), lambda i, f: (i, 0)),
            pl.BlockSpec((1, D_MODEL), lambda i, f: (0, 0)),
            pl.BlockSpec((D_MODEL, FFN_TF), lambda i, f: (0, f)),
            pl.BlockSpec((D_MODEL, FFN_TF), lambda i, f: (0, f + D_FF // FFN_TF)),
            pl.BlockSpec((FFN_TF, D_MODEL), lambda i, f: (f, 0)),
            pl.BlockSpec((1, D_MODEL), lambda i, f: (0, 0)),
        ],
        out_specs=pl.BlockSpec((tm, D_MODEL), lambda i, f: (i, 0)),
        scratch_shapes=[pltpu.VMEM((tm, D_MODEL), BF16), pltpu.VMEM((tm, D_MODEL), F32)],
        compiler_params=_cparams("parallel", "arbitrary"),
        name="ffn_half",
    )(x, pre_g, w_gu, w_gu, w_down, post_g)


def _absorb_kernel(wq_ref, wk_ref, o_ref):
    for h in range(MLA_HEADS):
        o_ref[:, h * LANES:(h + 1) * LANES] = _dot(wq_ref[h], wk_ref[h]).astype(BF16)


def _absorb(wq_nope, wk_t):
    return pl.pallas_call(
        _absorb_kernel,
        out_shape=jax.ShapeDtypeStruct((MLA_Q_LORA, MLA_HEADS * MLA_KV_LORA), BF16),
        name="absorb_uk",
    )(wq_nope, wk_t)


def _bias_from_dist(dist, table_ref, head):
    val = jnp.full(dist.shape, table_ref[0, head] * LOG2E, F32)
    for k in range(1, REL_BUCKETS):
        val = jnp.where(dist >= _BUCKET_THR[k], table_ref[k, head] * LOG2E, val)
    return jnp.where(dist < 0, MASK_VALUE, val)


def _bias_kernel(table_ref, bp_ref, bd_ref, bn_ref, *, past_len, dec_t):
    r = lax.broadcasted_iota(jnp.int32, (PROMPT_TQ, PROMPT_TK), 0)
    c = lax.broadcasted_iota(jnp.int32, (PROMPT_TQ, PROMPT_TK), 1)
    for head in range(DIFF_HEADS):
        for kind in range(2):
            bp_ref[head, kind] = _bias_from_dist(r - c + kind * PROMPT_TK, table_ref, head)
    t = lax.broadcasted_iota(jnp.int32, (dec_t, DEC_TK), 0)
    c = lax.broadcasted_iota(jnp.int32, (dec_t, DEC_TK), 1)
    tn = lax.broadcasted_iota(jnp.int32, (dec_t, LANES), 0)
    cn = lax.broadcasted_iota(jnp.int32, (dec_t, LANES), 1)
    for kvh in range(DIFF_KV_HEADS):
        for g in range(DIFF_GROUP):
            head = kvh * DIFF_GROUP + g
            far = _bias_from_dist(t - c + past_len, table_ref, head)
            last = _bias_from_dist(t - c + DEC_TK, table_ref, head)
            new = jnp.where(cn < dec_t, _bias_from_dist(tn - cn, table_ref, head), MASK_VALUE)
            for m in range(2):
                row = (g * 2 + m) * dec_t
                bd_ref[0, kvh, row:row + dec_t, :] = far
                bd_ref[1, kvh, row:row + dec_t, :] = last
                bn_ref[kvh, row:row + dec_t, :] = new


def _rel_bias_tiles(rel_table, past_len, dec_t):
    assert past_len - DEC_TK >= REL_FAR and dec_t <= LANES
    return pl.pallas_call(
        functools.partial(_bias_kernel, past_len=past_len, dec_t=dec_t),
        out_shape=(
            jax.ShapeDtypeStruct((DIFF_HEADS, 2, PROMPT_TQ, PROMPT_TK), F32),
            jax.ShapeDtypeStruct((2, DIFF_KV_HEADS, 4 * dec_t, DEC_TK), F32),
            jax.ShapeDtypeStruct((DIFF_KV_HEADS, 4 * dec_t, LANES), F32),
        ),
        in_specs=[pl.BlockSpec(memory_space=pltpu.SMEM)],
        name="rel_bias_tiles",
    )(rel_table)


def _mix_in_kernel(x_ref, pre_g_ref, w_in_ref, qg_ref, w_abs_ref, w_ra_ref, w_rb_ref, kvg_ref,
                   cos_ref, sin_ref,
                   q_ref, lat_t_ref, vx_ref, rows_ref, qd_ref, kd_t_ref, vdx_ref, kv_ref):
    tm = x_ref.shape[0]
    tk = PROMPT_TK
    h = _rms(x_ref[...], pre_g_ref[...]).astype(BF16)
    p = _dot(h, w_in_ref[...])
    cq = p[:, 0:256]
    ckv = p[:, 256:384]
    kr_a = p[:, 384:512]
    kr_b = p[:, 512:640]
    dq = p[:, 640:1152]
    dk = p[:, 1152:1408]
    dv = p[:, 1408:1664]
    cos = cos_ref[...]
    sin = sin_ref[...]

    cqn = _rms(cq, qg_ref[...]).astype(BF16)
    q_lat = _dot(cqn, w_abs_ref[...])
    r_a = _dot(cqn, w_ra_ref[...])
    r_b = _dot(cqn, w_rb_ref[...])
    for hd in range(MLA_HEADS):
        sl = slice(hd * LANES, (hd + 1) * LANES)
        q_ref[hd, :, 0:LANES] = (q_lat[:, sl] * MLA_QSCALE).astype(q_ref.dtype)
        q_ref[hd, :, LANES:QPAD] = ((r_a[:, sl] * cos + r_b[:, sl] * sin) * MLA_QSCALE).astype(q_ref.dtype)

    ckvn = _rms(ckv, kvg_ref[...])
    k_rope = kr_a * cos + kr_b * sin
    rows_ref[:, 0:MLA_KV_LORA] = ckvn
    rows_ref[:, MLA_KV_LORA:MLA_LAT] = k_rope[:, 0:MLA_ROPE]
    for c in range(tm // tk):
        rs = slice(c * tk, (c + 1) * tk)
        lat_t_ref[c, 0:LANES, :] = ckvn[rs].T.astype(BF16)
        lat_t_ref[c, LANES:QPAD, :] = k_rope[rs].T.astype(BF16)
        for kvh in range(DIFF_KV_HEADS):
            kd_t_ref[kvh, c] = dk[rs, kvh * LANES:(kvh + 1) * LANES].T.astype(BF16)
    ones = jnp.ones((tm, LANES), BF16)
    vx_ref[:, 0:LANES] = ckvn.astype(BF16)
    vx_ref[:, LANES:QPAD] = ones
    for kvh in range(DIFF_KV_HEADS):
        vdx_ref[:, 2 * kvh * LANES:(2 * kvh + 1) * LANES] = dv[:, kvh * LANES:(kvh + 1) * LANES].astype(BF16)
        vdx_ref[:, (2 * kvh + 1) * LANES:(2 * kvh + 2) * LANES] = ones

    lane = lax.broadcasted_iota(jnp.int32, (tm, LANES), 1)
    for kvh in range(DIFF_KV_HEADS):
        for g in range(DIFF_GROUP):
            pair = dq[:, (kvh * 2 + g) * LANES:(kvh * 2 + g + 1) * LANES] * DIFF_QSCALE
            for m in range(2):
                keep = (lane < DIFF_DH) if m == 0 else (lane >= DIFF_DH)
                qd_ref[kvh * 4 + g * 2 + m] = jnp.where(keep, pair, 0.0).astype(qd_ref.dtype)
    for j, piece in enumerate((dk[:, 0:LANES], dk[:, LANES:2 * LANES], dv[:, 0:LANES], dv[:, LANES:2 * LANES])):
        kv_ref[pl.ds(j, tm, stride=4), :] = piece


def _mix_in(x, wl, cos_t, sin_t, tm, q_dtype):
    n = x.shape[0]
    n_tab = cos_t.shape[0] // tm
    tk = PROMPT_TK
    const = lambda i: (0, 0)
    w = lambda a: pl.BlockSpec(a.shape, const)
    return pl.pallas_call(
        _mix_in_kernel,
        out_shape=(
            jax.ShapeDtypeStruct((MLA_HEADS, n, QPAD), q_dtype),
            jax.ShapeDtypeStruct((n // tk, QPAD, tk), BF16),
            jax.ShapeDtypeStruct((n, QPAD), BF16),
            jax.ShapeDtypeStruct((n, MLA_LAT), F32),
            jax.ShapeDtypeStruct((8, n, LANES), q_dtype),
            jax.ShapeDtypeStruct((DIFF_KV_HEADS, n // tk, LANES, tk), BF16),
            jax.ShapeDtypeStruct((n, 2 * DIFF_KV_HEADS * LANES), BF16),
            jax.ShapeDtypeStruct((4 * n, LANES), F32),
        ),
        grid=(n // tm,),
        in_specs=[
            pl.BlockSpec((tm, D_MODEL), lambda i: (i, 0)),
            w(wl['mix_pre_g']), w(wl['w_in_a']), w(wl['q_norm_g']), w(wl['w_abs']),
            w(wl['w_rope_a']), w(wl['w_rope_b']), w(wl['kv_norm_g']),
            pl.BlockSpec((tm, LANES), lambda i: (i % n_tab, 0)),
            pl.BlockSpec((tm, LANES), lambda i: (i % n_tab, 0)),
        ],
        out_specs=(
            pl.BlockSpec((MLA_HEADS, tm, QPAD), lambda i: (0, i, 0)),
            pl.BlockSpec((tm // tk, QPAD, tk), lambda i: (i, 0, 0)),
            pl.BlockSpec((tm, QPAD), lambda i: (i, 0)),
            pl.BlockSpec((tm, MLA_LAT), lambda i: (i, 0)),
            pl.BlockSpec((8, tm, LANES), lambda i: (0, i, 0)),
            pl.BlockSpec((DIFF_KV_HEADS, tm // tk, LANES, tk), lambda i: (0, i, 0, 0)),
            pl.BlockSpec((tm, 2 * DIFF_KV_HEADS * LANES), lambda i: (i, 0)),
            pl.BlockSpec((4 * tm, LANES), lambda i: (i, 0)),
        ),
        compiler_params=_cparams("parallel"),
        name="mix_in",
    )(x, wl['mix_pre_g'], wl['w_in_a'], wl['q_norm_g'], wl['w_abs'], wl['w_rope_a'], wl['w_rope_b'],
      wl['kv_norm_g'], cos_t, sin_t)


def _online_step(s, v, m, l, acc, shift=None, v_transposed=False):
    m_blk = jnp.max(s, axis=-1, keepdims=True)
    if shift is not None:
        m_blk = m_blk + shift
    m_new = jnp.maximum(m, m_blk)
    alpha = jnp.exp2(m - m_new)
    p = jnp.exp2(s - (m_new if shift is None else m_new - shift))
    l_new = alpha * l + jnp.sum(p, axis=-1, keepdims=True)
    pv = _dot_nt(p.astype(BF16), v) if v_transposed else _dot(p.astype(BF16), v)
    acc_new = alpha * acc + pv
    return m_new, l_new, acc_new


def _flash_step(s, vx, m_ref, acc_ref, shift=None):
    tiles = [s[:, t * LANES:(t + 1) * LANES] for t in range(s.shape[1] // LANES)]
    m_blk = jnp.max(functools.reduce(jnp.maximum, tiles), axis=-1, keepdims=True)
    m_prev = m_ref[...]
    m_new = jnp.maximum(m_prev, m_blk if shift is None else m_blk + shift)
    alpha = jnp.exp2(m_prev - m_new)
    sub = m_new if shift is None else m_new - shift
    p = jnp.concatenate([jnp.exp2(t - sub) for t in tiles], axis=1).astype(BF16)
    acc_ref[...] = jnp.concatenate([alpha, alpha], axis=1) * acc_ref[...] + _dot(p, vx)
    m_ref[...] = m_new


def _flash_init(m_ref, acc_ref):
    m_ref[...] = jnp.full_like(m_ref, -jnp.inf)
    acc_ref[...] = jnp.zeros_like(acc_ref)


def _mla_prompt_kernel(q_ref, kt_ref, vx_ref, o_ref, m_ref, acc_ref):
    i = pl.program_id(1)
    tq, tk = PROMPT_TQ, PROMPT_TK
    rows = MLA_HEADS * tq
    q = q_ref[...].reshape(rows, QPAD)
    _flash_init(m_ref, acc_ref)

    def chunk(j, mask):
        s = _dot(q, kt_ref[j])
        if mask is not None:
            s = jnp.where(mask, s, MASK_VALUE)
        _flash_step(s, vx_ref[pl.ds(pl.multiple_of(j * tk, tk), tk), :], m_ref, acc_ref)

    def body(j, carry):
        chunk(j, None)
        return carry

    lax.fori_loop(0, i, body, 0)
    r = lax.broadcasted_iota(jnp.int32, (MLA_HEADS, tq, tk), 1).reshape(rows, tk)
    c = lax.broadcasted_iota(jnp.int32, (rows, tk), 1)
    chunk(i, c <= r)

    for hd in range(MLA_HEADS):
        a = acc_ref[hd * tq:(hd + 1) * tq, :]
        o_ref[:, hd * LANES:(hd + 1) * LANES] = (a[:, 0:LANES] * (1.0 / a[:, LANES:QPAD])).astype(o_ref.dtype)


def _mla_prompt(q, lat_t, vx, batch, seq):
    tq, tk = PROMPT_TQ, PROMPT_TK
    assert tq == tk
    nq = seq // tq
    rows = MLA_HEADS * tq
    return pl.pallas_call(
        _mla_prompt_kernel,
        out_shape=jax.ShapeDtypeStruct((batch * seq, MLA_HEADS * MLA_KV_LORA), BF16),
        grid=(batch, nq),
        in_specs=[
            pl.BlockSpec((MLA_HEADS, tq, QPAD), lambda b, i: (0, b * nq + i, 0)),
            pl.BlockSpec((seq // tk, QPAD, tk), lambda b, i: (b, 0, 0)),
            pl.BlockSpec((seq, QPAD), lambda b, i: (b, 0)),
        ],
        out_specs=pl.BlockSpec((tq, MLA_HEADS * MLA_KV_LORA), lambda b, i: (b * nq + i, 0)),
        scratch_shapes=[pltpu.VMEM((rows, LANES), F32), pltpu.VMEM((rows, QPAD), F32)],
        compiler_params=_cparams("parallel", "arbitrary"),
        name="mla_prompt",
    )(q, lat_t, vx)


def _diff_lambda(lq1, lk1, lq2, lk2, lam_init):
    return (jnp.exp(jnp.sum(lq1 * lk1, axis=-1, keepdims=True))
            - jnp.exp(jnp.sum(lq2 * lk2, axis=-1, keepdims=True)) + lam_init)


def _diff_combine(o0, o1, lam, subln_g, lam_init):
    return _rms(o0 - lam * o1, subln_g) * (1.0 - lam_init)


def _diff_prompt_kernel(q_ref, kt_ref, vx_ref, bias_ref, far_ref, lq1_ref, lk1_ref, lq2_ref, lk2_ref,
                        subln_ref, o_ref, m_ref, acc_ref, *, lam_init):
    i = pl.program_id(2)
    tq, tk = PROMPT_TQ, PROMPT_TK
    rows = 4 * tq
    q = q_ref[...].reshape(rows, LANES)
    _flash_init(m_ref, acc_ref)

    def biased(j, kind):
        s = _dot(q, kt_ref[j])
        return (s.reshape(DIFF_GROUP, 2, tq, tk) + bias_ref[:, kind][:, None]).reshape(rows, tk)

    def pair(jj, carry):
        s = jnp.concatenate([_dot(q, kt_ref[2 * jj]), _dot(q, kt_ref[2 * jj + 1])], axis=1)
        _flash_step(s, vx_ref[pl.ds(pl.multiple_of(jj * 2 * tk, 2 * tk), 2 * tk), :], m_ref, acc_ref,
                    far_ref[...])
        return carry

    n_far = jnp.maximum(i - 1, 0)
    lax.fori_loop(0, n_far // 2, pair, 0)

    @pl.when(n_far % 2 == 1)
    def _():
        j = n_far - 1
        _flash_step(_dot(q, kt_ref[j]), vx_ref[pl.ds(pl.multiple_of(j * tk, tk), tk), :], m_ref, acc_ref,
                    far_ref[...])

    @pl.when(i > 0)
    def _():
        s = jnp.concatenate([biased(i - 1, 1), biased(i, 0)], axis=1)
        _flash_step(s, vx_ref[pl.ds(pl.multiple_of((i - 1) * tk, tk), 2 * tk), :], m_ref, acc_ref)

    @pl.when(i == 0)
    def _():
        _flash_step(biased(0, 0), vx_ref[0:tk, :], m_ref, acc_ref)

    lam = _diff_lambda(lq1_ref[...], lk1_ref[...], lq2_ref[...], lk2_ref[...], lam_init)
    acc = acc_ref[...]
    o = acc[:, 0:LANES] * (1.0 / acc[:, LANES:2 * LANES])
    for g in range(DIFF_GROUP):
        o0 = o[(2 * g) * tq:(2 * g + 1) * tq, :]
        o1 = o[(2 * g + 1) * tq:(2 * g + 2) * tq, :]
        o_ref[:, g * LANES:(g + 1) * LANES] = _diff_combine(
            o0, o1, lam, subln_ref[...], lam_init).astype(o_ref.dtype)


def _diff_prompt(qd, kd_t, vdx, bias_p, far_rows, wl, lam_init, batch, seq):
    tq, tk = PROMPT_TQ, PROMPT_TK
    nq = seq // tq
    rows = 4 * tq
    assert tq == tk and tk >= REL_FAR
    const3 = lambda b, h, i: (0, 0)
    small = lambda a: pl.BlockSpec(a.shape, const3)
    return pl.pallas_call(
        functools.partial(_diff_prompt_kernel, lam_init=lam_init),
        out_shape=jax.ShapeDtypeStruct((batch * seq, DIFF_HEADS * 2 * DIFF_DH), BF16),
        grid=(batch, DIFF_KV_HEADS, nq),
        in_specs=[
            pl.BlockSpec((4, tq, LANES), lambda b, h, i: (h, b * nq + i, 0)),
            pl.BlockSpec((None, seq // tk, LANES, tk), lambda b, h, i: (h, b, 0, 0)),
            pl.BlockSpec((seq, 2 * LANES), lambda b, h, i: (b, h)),
            pl.BlockSpec((DIFF_GROUP, 2, tq, tk), lambda b, h, i: (h, 0, 0, 0)),
            pl.BlockSpec((None, rows, LANES), lambda b, h, i: (h, 0, 0)),
            small(wl['lq1']), small(wl['lk1']), small(wl['lq2']), small(wl['lk2']), small(wl['subln_g']),
        ],
        out_specs=pl.BlockSpec((tq, DIFF_GROUP * LANES), lambda b, h, i: (b * nq + i, h)),
        scratch_shapes=[pltpu.VMEM((rows, LANES), F32), pltpu.VMEM((rows, 2 * LANES), F32)],
        compiler_params=_cparams("parallel", "parallel", "arbitrary"),
        name="diff_prompt",
    )(qd, kd_t, vdx, bias_p, far_rows, wl['lq1'], wl['lk1'], wl['lq2'], wl['lk2'], wl['subln_g'])


def _decode_kernel(pt_ref, qm_ref, qd_ref, rows_ref, kv_ref, bias_ref, biasn_ref,
                   lq1_ref, lk1_ref, lq2_ref, lk2_ref, subln_ref, cm_hbm, cd_hbm,
                   om_ref, od_ref, mbuf, dbuf, nlat, nkv, sem,
                   *, lam_init, layer, n_chunks, dec_t):
    b = pl.program_id(0)
    nb = pl.num_programs(0)
    mrows = MLA_HEADS * dec_t
    drows = 4 * dec_t
    prow = PAGE_SIZE * 4

    def copies(g):
        bb, c, slot = g // n_chunks, g % n_chunks, g % DEC_SLOTS
        out = []
        for p in range(DEC_PAGES):
            phys = pt_ref[bb, c * DEC_PAGES + p]
            out.append(pltpu.make_async_copy(cm_hbm.at[layer, phys],
                                             mbuf.at[slot, :, pl.ds(p * PAGE_SIZE, PAGE_SIZE)], sem.at[0, slot]))
            out.append(pltpu.make_async_copy(cd_hbm.at[layer, phys], dbuf.at[slot, pl.ds(p * prow, prow)],
                                             sem.at[1, slot]))
        return out

    def start(g):
        for n, cp in enumerate(copies(g)):
            cp.start(priority=(n // 2) % 2)

    @pl.when(b == 0)
    def _():
        nlat[...] = jnp.zeros_like(nlat)
        nkv[...] = jnp.zeros_like(nkv)
        for g0 in range(DEC_SLOTS - 1):
            start(g0)

    qm = qm_ref[...].reshape(mrows, QPAD).astype(BF16)
    qd = [qd_ref[kvh * 4:(kvh + 1) * 4].reshape(drows, LANES).astype(BF16) for kvh in range(DIFF_KV_HEADS)]

    def init(r, d):
        return (jnp.full((r, 1), -jnp.inf, F32), jnp.zeros((r, 1), F32), jnp.zeros((r, d), F32))

    def body(c, carry):
        st_m, st_d = carry
        g = b * n_chunks + c
        slot = g % DEC_SLOTS

        @pl.when(g + DEC_SLOTS - 1 < nb * n_chunks)
        def _():
            start(g + DEC_SLOTS - 1)

        for cp in copies(g):
            cp.wait()

        kt = mbuf[slot].astype(BF16)
        st_m = _online_step(_dot(qm[:, 0:MLA_LAT], kt), kt[0:MLA_KV_LORA, :], *st_m, v_transposed=True)

        kind = lax.convert_element_type(c == n_chunks - 1, jnp.int32)
        new_d = []
        for kvh in range(DIFF_KV_HEADS):
            kd = dbuf[slot, pl.ds(kvh, DEC_TK, stride=4), :].astype(BF16)
            vd = dbuf[slot, pl.ds(2 + kvh, DEC_TK, stride=4), :].astype(BF16)
            s = _dot_nt(qd[kvh], kd) + bias_ref[kind, kvh]
            new_d.append(_online_step(s, vd, *st_d[kvh]))
        return st_m, tuple(new_d)

    st_m, st_d = lax.fori_loop(
        0, n_chunks, body,
        (init(mrows, MLA_KV_LORA), tuple(init(drows, LANES) for _ in range(DIFF_KV_HEADS))))

    nlat[0:dec_t, 0:MLA_KV_LORA] = rows_ref[:, 0:MLA_KV_LORA]
    nlat[0:dec_t, MLA_KV_LORA:MLA_LAT] = rows_ref[:, MLA_KV_LORA:MLA_LAT]
    for j in range(4):
        nkv[0:dec_t, j * LANES:(j + 1) * LANES] = kv_ref[pl.ds(j, dec_t, stride=4), :]
    new_k = nlat[...].astype(BF16)
    rr = lax.broadcasted_iota(jnp.int32, (MLA_HEADS, dec_t, LANES), 1).reshape(mrows, LANES)
    cc = lax.broadcasted_iota(jnp.int32, (mrows, LANES), 1)
    s = jnp.where(cc <= rr, _dot_nt(qm, new_k), MASK_VALUE)
    _, l_m, acc_m = _online_step(s, new_k[:, 0:MLA_KV_LORA], *st_m)
    o_m = acc_m * (1.0 / l_m)
    for hd in range(MLA_HEADS):
        om_ref[:, hd * LANES:(hd + 1) * LANES] = o_m[hd * dec_t:(hd + 1) * dec_t, :]

    lam = _diff_lambda(lq1_ref[...], lk1_ref[...], lq2_ref[...], lk2_ref[...], lam_init)
    for kvh in range(DIFF_KV_HEADS):
        kd = nkv[:, kvh * LANES:(kvh + 1) * LANES].astype(BF16)
        vd = nkv[:, (2 + kvh) * LANES:(3 + kvh) * LANES].astype(BF16)
        s = _dot_nt(qd[kvh], kd) + biasn_ref[kvh]
        _, l_d, acc_d = _online_step(s, vd, *st_d[kvh])
        o = acc_d * (1.0 / l_d)
        for g in range(DIFF_GROUP):
            o0 = o[(2 * g) * dec_t:(2 * g + 1) * dec_t, :]
            o1 = o[(2 * g + 1) * dec_t:(2 * g + 2) * dec_t, :]
            od_ref[:, (kvh * 2 + g) * LANES:(kvh * 2 + g + 1) * LANES] = _diff_combine(
                o0, o1, lam, subln_ref[...], lam_init)


def _decode(page_table, qm, qd, rows, kv, bias_d, bias_n, wl, cache_mla, cache_diff,
            lam_init, layer, dec_b, dec_t):
    n_pages = page_table.shape[1]
    assert n_pages % DEC_PAGES == 0 and dec_b * (n_pages // DEC_PAGES) >= DEC_SLOTS
    n_chunks = n_pages // DEC_PAGES
    n_pool = cache_diff.shape[1]
    cd = cache_diff.reshape(cache_diff.shape[0], n_pool, PAGE_SIZE * 4, LANES)
    cm = jnp.swapaxes(cache_mla, 2, 3)
    const = lambda b, pt: (0, 0)
    small = lambda a: pl.BlockSpec(a.shape, const)
    grid_spec = pltpu.PrefetchScalarGridSpec(
        num_scalar_prefetch=1,
        grid=(dec_b,),
        in_specs=[
            pl.BlockSpec((MLA_HEADS, dec_t, QPAD), lambda b, pt: (0, b, 0)),
            pl.BlockSpec((8, dec_t, LANES), lambda b, pt: (0, b, 0)),
            pl.BlockSpec((dec_t, MLA_LAT), lambda b, pt: (b, 0)),
            pl.BlockSpec((4 * dec_t, LANES), lambda b, pt: (b, 0)),
            pl.BlockSpec(bias_d.shape, lambda b, pt: (0, 0, 0, 0)),
            pl.BlockSpec(bias_n.shape, lambda b, pt: (0, 0, 0)),
            small(wl['lq1']), small(wl['lk1']), small(wl['lq2']), small(wl['lk2']), small(wl['subln_g']),
            pl.BlockSpec(memory_space=pl.ANY),
            pl.BlockSpec(memory_space=pl.ANY),
        ],
        out_specs=(
            pl.BlockSpec((dec_t, MLA_HEADS * MLA_KV_LORA), lambda b, pt: (b, 0)),
            pl.BlockSpec((dec_t, DIFF_HEADS * LANES), lambda b, pt: (b, 0)),
        ),
        scratch_shapes=[
            pltpu.VMEM((DEC_SLOTS, MLA_LAT, DEC_TK), F32),
            pltpu.VMEM((DEC_SLOTS, DEC_TK * 4, LANES), F32),
            pltpu.VMEM((LANES, QPAD), F32),
            pltpu.VMEM((LANES, 4 * LANES), F32),
            pltpu.SemaphoreType.DMA((2, DEC_SLOTS)),
        ],
    )
    return pl.pallas_call(
        functools.partial(_decode_kernel, lam_init=lam_init, layer=layer, n_chunks=n_chunks, dec_t=dec_t),
        out_shape=(
            jax.ShapeDtypeStruct((dec_b * dec_t, MLA_HEADS * MLA_KV_LORA), F32),
            jax.ShapeDtypeStruct((dec_b * dec_t, DIFF_HEADS * LANES), F32),
        ),
        grid_spec=grid_spec,
        compiler_params=_cparams("arbitrary"),
        name="paged_decode",
    )(page_table, qm, qd, rows, kv, bias_d, bias_n,
      wl['lq1'], wl['lk1'], wl['lq2'], wl['lk2'], wl['subln_g'], cm, cd)


def _merge_kernel(x_ref, om_ref, od_ref, pre_g_ref, wg_ref, wuv_ref, woa_ref, wob_ref, wout_ref,
                  post_g_ref, o_ref):
    x = x_ref[...]
    h = _rms(x, pre_g_ref[...]).astype(BF16)
    gates = _dot(h, wg_ref[...])
    v_a = _dot(om_ref[...].astype(BF16), wuv_ref[...])
    y_a = _dot(v_a.astype(BF16), woa_ref[...])
    y_b = _dot(od_ref[...].astype(BF16), wob_ref[...])
    z = jax.nn.sigmoid(gates[:, 0:D_MODEL]) * y_a + jax.nn.sigmoid(gates[:, D_MODEL:2 * D_MODEL]) * y_b
    y = _dot(z.astype(BF16), wout_ref[...])
    o_ref[...] = x + _rms(y, post_g_ref[...])


def _merge(x, o_m, o_d, wl, tm):
    n = x.shape[0]
    const = lambda i: (0, 0)
    w = lambda a: pl.BlockSpec(a.shape, const)
    row = lambda width: pl.BlockSpec((tm, width), lambda i: (i, 0))
    return pl.pallas_call(
        _merge_kernel,
        out_shape=jax.ShapeDtypeStruct((n, D_MODEL), F32),
        grid=(n // tm,),
        in_specs=[row(D_MODEL), row(o_m.shape[1]), row(o_d.shape[1]),
                  w(wl['mix_pre_g']), w(wl['w_gates']), w(wl['w_uv_bd']), w(wl['w_o_mla']),
                  w(wl['w_o_diff']), w(wl['w_out']), w(wl['mix_post_g'])],
        out_specs=row(D_MODEL),
        compiler_params=_cparams("parallel"),
        name="merge",
    )(x, o_m, o_d, wl['mix_pre_g'], wl['w_gates'], wl['w_uv_bd'], wl['w_o_mla'], wl['w_o_diff'],
      wl['w_out'], wl['mix_post_g'])


def _rope_tables(pos, n_rows):
    half = MLA_ROPE // 2
    inv_freq = ROPE_THETA ** (-jnp.arange(half, dtype=F32) / half)
    ang = pos.astype(F32)[:, None] * inv_freq[None, :]
    cos, sin = jnp.cos(ang), jnp.sin(ang)
    zeros = jnp.zeros((pos.shape[0], LANES - MLA_ROPE), F32)
    cos_t = jnp.concatenate([cos, cos, zeros], axis=1)
    sin_t = jnp.concatenate([-sin, sin, zeros], axis=1)
    reps = n_rows // pos.shape[0]
    return jnp.tile(cos_t, (reps, 1)), jnp.tile(sin_t, (reps, 1))


def _layer_weights(layer, p):
    g = lambda name: p[name][layer].reshape(1, -1)
    w_in = p['w_in'][layer]
    o = np.cumsum((MLA_Q_LORA, MLA_KV_LORA, MLA_ROPE, DIFF_HEADS * 2 * DIFF_DH,
                   DIFF_KV_HEADS * 2 * DIFF_DH, DIFF_KV_HEADS * 2 * DIFF_DH)).tolist()
    half = MLA_ROPE // 2
    kr = w_in[:, o[1]:o[2]]
    zpad = jnp.zeros((D_MODEL, LANES - MLA_ROPE), F32)
    kr_a = jnp.concatenate([kr, zpad], axis=1)
    kr_b = jnp.concatenate([kr[:, half:], kr[:, :half], zpad], axis=1)
    w_in_a = jnp.concatenate([w_in[:, :o[1]], kr_a, kr_b, w_in[:, o[2]:o[5]]], axis=1).astype(BF16)

    w_uq = p['w_uq'][layer].reshape(MLA_Q_LORA, MLA_HEADS, MLA_NOPE + MLA_ROPE)
    wq_nope = jnp.moveaxis(w_uq[:, :, :MLA_NOPE], 1, 0).astype(BF16)
    wk_t = jnp.transpose(p['w_uk'][layer], (1, 2, 0)).astype(BF16)
    x1 = w_uq[:, :, MLA_NOPE:MLA_NOPE + half]
    x2 = w_uq[:, :, MLA_NOPE + half:]
    hz = jnp.zeros((MLA_Q_LORA, MLA_HEADS, LANES - MLA_ROPE), F32)
    w_rope_a = jnp.concatenate([x1, x2, hz], axis=2).reshape(MLA_Q_LORA, MLA_HEADS * LANES).astype(BF16)
    w_rope_b = jnp.concatenate([x2, x1, hz], axis=2).reshape(MLA_Q_LORA, MLA_HEADS * LANES).astype(BF16)

    eye = jnp.eye(MLA_HEADS, dtype=F32)
    w_uv_bd = jnp.einsum('lhv,hg->hlgv', p['w_uv'][layer], eye).reshape(
        MLA_HEADS * MLA_KV_LORA, MLA_HEADS * MLA_V).astype(BF16)
    return {
        'ffn1_pre_g': g('ffn1_pre_g'), 'ffn1_w_gu': p['ffn1_w_gu'][layer].astype(BF16),
        'ffn1_w_down': p['ffn1_w_down'][layer].astype(BF16), 'ffn1_post_g': g('ffn1_post_g'),
        'ffn2_pre_g': g('ffn2_pre_g'), 'ffn2_w_gu': p['ffn2_w_gu'][layer].astype(BF16),
        'ffn2_w_down': p['ffn2_w_down'][layer].astype(BF16), 'ffn2_post_g': g('ffn2_post_g'),
        'mix_pre_g': g('mix_pre_g'), 'w_in_a': w_in_a, 'w_gates': w_in[:, o[5]:].astype(BF16),
        'q_norm_g': g('mla_q_norm_g'), 'kv_norm_g': g('mla_kv_norm_g'),
        'w_abs': _absorb(wq_nope, wk_t), 'w_rope_a': w_rope_a, 'w_rope_b': w_rope_b,
        'w_uv_bd': w_uv_bd, 'w_o_mla': p['w_o_mla'][layer].astype(BF16),
        'w_o_diff': p['w_o_diff'][layer].astype(BF16), 'w_out': p['w_out'][layer].astype(BF16),
        'mix_post_g': g('mix_post_g'),
        'lq1': g('diff_lq1'), 'lk1': g('diff_lk1'), 'lq2': g('diff_lq2'), 'lk2': g('diff_lk2'),
        'subln_g': g('diff_subln_g'),
    }


def _far_rows(rel_table, rows_per_map):
    far = (rel_table[REL_BUCKETS - 1].astype(F32) * LOG2E).reshape(DIFF_KV_HEADS, DIFF_GROUP, 1, 1)
    return jnp.broadcast_to(far, (DIFF_KV_HEADS, DIFF_GROUP, 2 * rows_per_map, LANES)).reshape(
        DIFF_KV_HEADS, 4 * rows_per_map, LANES)


def kernel(x_prompt, x_sample, cache_mla, cache_diff, page_table, ffn1_pre_g, ffn1_w_gu, ffn1_w_down, ffn1_post_g, mix_pre_g, w_in, mla_q_norm_g, w_uq, mla_kv_norm_g, w_uk, w_uv, diff_lq1, diff_lk1, diff_lq2, diff_lk2, diff_subln_g, rel_table, w_o_mla, w_o_diff, w_out, mix_post_g, ffn2_pre_g, ffn2_w_gu, ffn2_w_down, ffn2_post_g):
    params = dict(
        ffn1_pre_g=ffn1_pre_g, ffn1_w_gu=ffn1_w_gu, ffn1_w_down=ffn1_w_down, ffn1_post_g=ffn1_post_g,
        mix_pre_g=mix_pre_g, w_in=w_in, mla_q_norm_g=mla_q_norm_g, w_uq=w_uq,
        mla_kv_norm_g=mla_kv_norm_g, w_uk=w_uk, w_uv=w_uv, diff_lq1=diff_lq1, diff_lk1=diff_lk1,
        diff_lq2=diff_lq2, diff_lk2=diff_lk2, diff_subln_g=diff_subln_g, w_o_mla=w_o_mla,
        w_o_diff=w_o_diff, w_out=w_out, mix_post_g=mix_post_g, ffn2_pre_g=ffn2_pre_g,
        ffn2_w_gu=ffn2_w_gu, ffn2_w_down=ffn2_w_down, ffn2_post_g=ffn2_post_g)
    batch, seq, _ = x_prompt.shape
    dec_b, dec_t, _ = x_sample.shape
    depth = cache_mla.shape[0]
    past_len = page_table.shape[1] * PAGE_SIZE
    n_p, n_s = batch * seq, dec_b * dec_t
    tm_p, tm_s = 512, 512

    cos_p, sin_p = _rope_tables(jnp.arange(seq), seq)
    cos_s, sin_s = _rope_tables(past_len + jnp.arange(dec_t), tm_s)
    bias_p, bias_d, bias_n = _rel_bias_tiles(rel_table.astype(F32), past_len, dec_t)
    far_p = _far_rows(rel_table, PROMPT_TQ)

    y_p = x_prompt.reshape(n_p, D_MODEL)
    y_s = x_sample.reshape(n_s, D_MODEL)
    outs = ([], [], [], [])
    for layer in range(depth):
        wl = _layer_weights(layer, params)
        lam_init = 0.8 - 0.6 * math.exp(-0.3 * layer)

        y_p = _ffn(y_p, wl['ffn1_pre_g'], wl['ffn1_w_gu'], wl['ffn1_w_down'], wl['ffn1_post_g'], 1024)
        q, lat_t, vx, rows_p, qd, kd_t, vdx, kv_p = _mix_in(y_p, wl, cos_p, sin_p, tm_p, BF16)
        o_m = _mla_prompt(q, lat_t, vx, batch, seq)
        o_d = _diff_prompt(qd, kd_t, vdx, bias_p, far_p, wl, lam_init, batch, seq)
        y_p = _merge(y_p, o_m, o_d, wl, tm_p)
        y_p = _ffn(y_p, wl['ffn2_pre_g'], wl['ffn2_w_gu'], wl['ffn2_w_down'], wl['ffn2_post_g'], 1024)

        y_s = _ffn(y_s, wl['ffn1_pre_g'], wl['ffn1_w_gu'], wl['ffn1_w_down'], wl['ffn1_post_g'], tm_s)
        q, _, _, rows_s, qd, _, _, kv_s = _mix_in(y_s, wl, cos_s, sin_s, tm_s, F32)
        o_m, o_d = _decode(page_table, q, qd, rows_s, kv_s, bias_d, bias_n, wl,
                           cache_mla, cache_diff, lam_init, layer, dec_b, dec_t)
        y_s = _merge(y_s, o_m, o_d, wl, tm_s)
        y_s = _ffn(y_s, wl['ffn2_pre_g'], wl['ffn2_w_gu'], wl['ffn2_w_down'], wl['ffn2_post_g'], tm_s)

        outs[0].append(rows_p.reshape(batch, seq, MLA_LAT))
        outs[1].append(kv_p.reshape(batch, seq, 2, DIFF_KV_HEADS, 2 * DIFF_DH))
        outs[2].append(rows_s.reshape(dec_b, dec_t, MLA_LAT))
        outs[3].append(kv_s.reshape(dec_b, dec_t, 2, DIFF_KV_HEADS, 2 * DIFF_DH))
    return (y_p.reshape(batch, seq, D_MODEL), y_s.reshape(dec_b, dec_t, D_MODEL),
            jnp.stack(outs[0]), jnp.stack(outs[1]), jnp.stack(outs[2]), jnp.stack(outs[3]))
```

```python
import functools
import math

import numpy as np
import jax
import jax.numpy as jnp
from jax import lax
from jax.experimental import pallas as pl
from jax.experimental.pallas import tpu as pltpu

F32 = jnp.float32
BF16 = jnp.bfloat16

D_MODEL = 1024
MLA_HEADS = 8
MLA_NOPE = 64
MLA_ROPE = 32
MLA_V = 64
MLA_Q_LORA = D_MODEL // 4
MLA_KV_LORA = D_MODEL // 8
MLA_LAT = MLA_KV_LORA + MLA_ROPE
MLA_SCALE = (MLA_NOPE + MLA_ROPE) ** -0.5
DIFF_HEADS = 4
DIFF_KV_HEADS = 2
DIFF_GROUP = DIFF_HEADS // DIFF_KV_HEADS
DIFF_DH = 64
DIFF_SCALE = DIFF_DH ** -0.5
REL_BUCKETS = 32
REL_MAX_DIST = 128
D_FF = ((8 * D_MODEL // 3 + 127) // 128) * 128
ROPE_THETA = 10000.0
NORM_EPS = 1e-6
MASK_VALUE = -1e30
PAGE_SIZE = 128
LOG2E = math.log2(math.e)
MLA_QSCALE = MLA_SCALE * LOG2E
DIFF_QSCALE = DIFF_SCALE * LOG2E

LANES = 128
QPAD = 2 * LANES
VMEM_LIMIT = 56 * 1024 * 1024

FFN_TF = 256
FFN_TM = 1024
PROMPT_TQ = 256
MLA_TQ = 512
PROMPT_TK = 256
DEC_PAGES = 32
DEC_TK = DEC_PAGES * PAGE_SIZE
DEC_SLOTS = 3


def _bucket_thresholds():
    max_exact = REL_BUCKETS // 2
    n = np.arange(0, 4 * REL_MAX_DIST)
    large = max_exact + (np.log(np.maximum(n, 1).astype(np.float32) / max_exact)
                         / math.log(REL_MAX_DIST / max_exact)
                         * (REL_BUCKETS - max_exact)).astype(np.int32)
    bucket = np.where(n < max_exact, n, np.minimum(large, REL_BUCKETS - 1))
    assert np.all(np.diff(bucket) >= 0)
    return [int(np.argmax(bucket >= k)) for k in range(REL_BUCKETS)]


_BUCKET_THR = _bucket_thresholds()
REL_FAR = _BUCKET_THR[REL_BUCKETS - 1]


def _rms(x, g):
    return x * lax.rsqrt(jnp.mean(x * x, axis=-1, keepdims=True) + NORM_EPS) * g


def _dot(a, b):
    return jnp.dot(a, b, preferred_element_type=F32)


def _dot_nt(a, b):
    return lax.dot_general(a, b, (((1,), (1,)), ((), ())), preferred_element_type=F32)


def _cparams(*sem):
    return pltpu.CompilerParams(dimension_semantics=sem, vmem_limit_bytes=VMEM_LIMIT)


def _ffn_kernel(x_ref, pre_g_ref, wg_ref, wu_ref, wd_ref, post_g_ref, o_ref, h_ref, acc_ref):
    f = pl.program_id(1)

    @pl.when(f == 0)
    def _():
        h_ref[...] = _rms(x_ref[...], pre_g_ref[...]).astype(BF16)
        acc_ref[...] = jnp.zeros_like(acc_ref)

    h = h_ref[...]
    gate = _dot(h, wg_ref[...])
    up = _dot(h, wu_ref[...])
    act = (gate * jax.nn.sigmoid(gate) * up).astype(BF16)
    acc_ref[...] += _dot(act, wd_ref[...])

    @pl.when(f == pl.num_programs(1) - 1)
    def _():
        o_ref[...] = x_ref[...] + 0.5 * _rms(acc_ref[...], post_g_ref[...])


def _ffn(x, pre_g, w_gu, w_down, post_g, tm):
    n = x.shape[0]
    nf = D_FF // FFN_TF
    return pl.pallas_call(
        _ffn_kernel,
        out_shape=jax.ShapeDtypeStruct((n, D_MODEL), F32),
        grid=(n // tm, nf),
        in_specs=[
            pl.BlockSpec((tm, D_MODEL), lambda i, f: (i, 0)),
            pl.BlockSpec((1, D_MODEL), lambda i, f: (0, 0)),
            pl.BlockSpec((D_MODEL, FFN_TF), lambda i, f: (0, f)),
            pl.BlockSpec((D_MODEL, FFN_TF), lambda i, f: (0, f + D_FF // FFN_TF)),
            pl.BlockSpec((FFN_TF, D_MODEL), lambda i, f: (f, 0)),
            pl.BlockSpec((1, D_MODEL), lambda i, f: (0, 0)),
        ],
        out_specs=pl.BlockSpec((tm, D_MODEL), lambda i, f: (i, 0)),
        scratch_shapes=[pltpu.VMEM((tm, D_MODEL), BF16), pltpu.VMEM((tm, D_MODEL), F32)],
        compiler_params=_cparams("parallel", "arbitrary"),
        name="ffn_half",
    )(x, pre_g, w_gu, w_gu, w_down, post_g)


def _absorb_kernel(wq_ref, wk_ref, o_ref):
    for h in range(MLA_HEADS):
        o_ref[:, h * LANES:(h + 1) * LANES] = _dot(wq_ref[h], wk_ref[h]).astype(BF16)


def _absorb(wq_nope, wk_t):
    return pl.pallas_call(
        _absorb_kernel,
        out_shape=jax.ShapeDtypeStruct((MLA_Q_LORA, MLA_HEADS * MLA_KV_LORA), BF16),
        name="absorb_uk",
    )(wq_nope, wk_t)


def _bias_from_dist(dist, table_ref, head):
    val = jnp.full(dist.shape, table_ref[0, head] * LOG2E, F32)
    for k in range(1, REL_BUCKETS):
        val = jnp.where(dist >= _BUCKET_THR[k], table_ref[k, head] * LOG2E, val)
    return jnp.where(dist < 0, MASK_VALUE, val)


def _bias_kernel(table_ref, bp_ref, bd_ref, bn_ref, *, past_len, dec_t):
    r = lax.broadcasted_iota(jnp.int32, (PROMPT_TQ, PROMPT_TK), 0)
    c = lax.broadcasted_iota(jnp.int32, (PROMPT_TQ, PROMPT_TK), 1)
    for head in range(DIFF_HEADS):
        for kind in range(2):
            bp_ref[head, kind] = _bias_from_dist(r - c + kind * PROMPT_TK, table_ref, head)
    t = lax.broadcasted_iota(jnp.int32, (dec_t, DEC_TK), 0)
    c = lax.broadcasted_iota(jnp.int32, (dec_t, DEC_TK), 1)
    tn = lax.broadcasted_iota(jnp.int32, (dec_t, LANES), 0)
    cn = lax.broadcasted_iota(jnp.int32, (dec_t, LANES), 1)
    for kvh in range(DIFF_KV_HEADS):
        for g in range(DIFF_GROUP):
            head = kvh * DIFF_GROUP + g
            far = _bias_from_dist(t - c + past_len, table_ref, head)
            last = _bias_from_dist(t - c + DEC_TK, table_ref, head)
            new = jnp.where(cn < dec_t, _bias_from_dist(tn - cn, table_ref, head), MASK_VALUE)
            for m in range(2):
                row = (g * 2 + m) * dec_t
                bd_ref[0, kvh, row:row + dec_t, :] = far
                bd_ref[1, kvh, row:row + dec_t, :] = last
                bn_ref[kvh, row:row + dec_t, :] = new


def _rel_bias_tiles(rel_table, past_len, dec_t):
    assert past_len - DEC_TK >= REL_FAR and dec_t <= LANES
    return pl.pallas_call(
        functools.partial(_bias_kernel, past_len=past_len, dec_t=dec_t),
        out_shape=(
            jax.ShapeDtypeStruct((DIFF_HEADS, 2, PROMPT_TQ, PROMPT_TK), F32),
            jax.ShapeDtypeStruct((2, DIFF_KV_HEADS, 4 * dec_t, DEC_TK), F32),
            jax.ShapeDtypeStruct((DIFF_KV_HEADS, 4 * dec_t, LANES), F32),
        ),
        in_specs=[pl.BlockSpec(memory_space=pltpu.SMEM)],
        name="rel_bias_tiles",
    )(rel_table)


def _mix_in_kernel(x_ref, pre_g_ref, w_in_ref, qg_ref, w_abs_ref, w_ra_ref, w_rb_ref, kvg_ref,
                   cos_ref, sin_ref,
                   q_ref, lat_t_ref, vx_ref, rows_ref, qd_ref, kd_t_ref, vdx_ref, kv_ref):
    tm = x_ref.shape[0]
    tk = PROMPT_TK
    h = _rms(x_ref[...], pre_g_ref[...]).astype(BF16)
    p = _dot(h, w_in_ref[...])
    cq = p[:, 0:256]
    ckv = p[:, 256:384]
    kr_a = p[:, 384:512]
    kr_b = p[:, 512:640]
    dq = p[:, 640:1152]
    dk = p[:, 1152:1408]
    dv = p[:, 1408:1664]
    cos = cos_ref[...]
    sin = sin_ref[...]

    cqn = _rms(cq, qg_ref[...]).astype(BF16)
    q_lat = _dot(cqn, w_abs_ref[...])
    r_a = _dot(cqn, w_ra_ref[...])
    r_b = _dot(cqn, w_rb_ref[...])
    for hd in range(MLA_HEADS):
        sl = slice(hd * LANES, (hd + 1) * LANES)
        q_ref[hd, :, 0:LANES] = (q_lat[:, sl] * MLA_QSCALE).astype(q_ref.dtype)
        q_ref[hd, :, LANES:QPAD] = ((r_a[:, sl] * cos + r_b[:, sl] * sin) * MLA_QSCALE).astype(q_ref.dtype)

    ckvn = _rms(ckv, kvg_ref[...])
    k_rope = kr_a * cos + kr_b * sin
    rows_ref[:, 0:MLA_KV_LORA] = ckvn
    rows_ref[:, MLA_KV_LORA:MLA_LAT] = k_rope[:, 0:MLA_ROPE]
    for c in range(tm // tk):
        rs = slice(c * tk, (c + 1) * tk)
        lat_t_ref[c, 0:LANES, :] = ckvn[rs].T.astype(BF16)
        lat_t_ref[c, LANES:QPAD, :] = k_rope[rs].T.astype(BF16)
        for kvh in range(DIFF_KV_HEADS):
            kd_t_ref[kvh, c] = dk[rs, kvh * LANES:(kvh + 1) * LANES].T.astype(BF16)
    ones = jnp.ones((tm, LANES), BF16)
    vx_ref[:, 0:LANES] = ckvn.astype(BF16)
    vx_ref[:, LANES:QPAD] = ones
    for kvh in range(DIFF_KV_HEADS):
        vdx_ref[:, 2 * kvh * LANES:(2 * kvh + 1) * LANES] = dv[:, kvh * LANES:(kvh + 1) * LANES].astype(BF16)
        vdx_ref[:, (2 * kvh + 1) * LANES:(2 * kvh + 2) * LANES] = ones

    lane = lax.broadcasted_iota(jnp.int32, (tm, LANES), 1)
    for kvh in range(DIFF_KV_HEADS):
        for g in range(DIFF_GROUP):
            pair = dq[:, (kvh * 2 + g) * LANES:(kvh * 2 + g + 1) * LANES] * DIFF_QSCALE
            for m in range(2):
                keep = (lane < DIFF_DH) if m == 0 else (lane >= DIFF_DH)
                qd_ref[kvh * 4 + g * 2 + m] = jnp.where(keep, pair, 0.0).astype(qd_ref.dtype)
    for j, piece in enumerate((dk[:, 0:LANES], dk[:, LANES:2 * LANES], dv[:, 0:LANES], dv[:, LANES:2 * LANES])):
        kv_ref[pl.ds(j, tm, stride=4), :] = piece


def _mix_in(x, wl, cos_t, sin_t, tm, q_dtype):
    n = x.shape[0]
    n_tab = cos_t.shape[0] // tm
    tk = PROMPT_TK
    const = lambda i: (0, 0)
    w = lambda a: pl.BlockSpec(a.shape, const)
    return pl.pallas_call(
        _mix_in_kernel,
        out_shape=(
            jax.ShapeDtypeStruct((MLA_HEADS, n, QPAD), q_dtype),
            jax.ShapeDtypeStruct((n // tk, QPAD, tk), BF16),
            jax.ShapeDtypeStruct((n, QPAD), BF16),
            jax.ShapeDtypeStruct((n, MLA_LAT), F32),
            jax.ShapeDtypeStruct((8, n, LANES), q_dtype),
            jax.ShapeDtypeStruct((DIFF_KV_HEADS, n // tk, LANES, tk), BF16),
            jax.ShapeDtypeStruct((n, 2 * DIFF_KV_HEADS * LANES), BF16),
            jax.ShapeDtypeStruct((4 * n, LANES), F32),
        ),
        grid=(n // tm,),
        in_specs=[
            pl.BlockSpec((tm, D_MODEL), lambda i: (i, 0)),
            w(wl['mix_pre_g']), w(wl['w_in_a']), w(wl['q_norm_g']), w(wl['w_abs']),
            w(wl['w_rope_a']), w(wl['w_rope_b']), w(wl['kv_norm_g']),
            pl.BlockSpec((tm, LANES), lambda i: (i % n_tab, 0)),
            pl.BlockSpec((tm, LANES), lambda i: (i % n_tab, 0)),
        ],
        out_specs=(
            pl.BlockSpec((MLA_HEADS, tm, QPAD), lambda i: (0, i, 0)),
            pl.BlockSpec((tm // tk, QPAD, tk), lambda i: (i, 0, 0)),
            pl.BlockSpec((tm, QPAD), lambda i: (i, 0)),
            pl.BlockSpec((tm, MLA_LAT), lambda i: (i, 0)),
            pl.BlockSpec((8, tm, LANES), lambda i: (0, i, 0)),
            pl.BlockSpec((DIFF_KV_HEADS, tm // tk, LANES, tk), lambda i: (0, i, 0, 0)),
            pl.BlockSpec((tm, 2 * DIFF_KV_HEADS * LANES), lambda i: (i, 0)),
            pl.BlockSpec((4 * tm, LANES), lambda i: (i, 0)),
        ),
        compiler_params=_cparams("parallel"),
        name="mix_in",
    )(x, wl['mix_pre_g'], wl['w_in_a'], wl['q_norm_g'], wl['w_abs'], wl['w_rope_a'], wl['w_rope_b'],
      wl['kv_norm_g'], cos_t, sin_t)


def _online_step(s, v, m, l, acc, shift=None, v_transposed=False):
    m_blk = jnp.max(s, axis=-1, keepdims=True)
    if shift is not None:
        m_blk = m_blk + shift
    m_new = jnp.maximum(m, m_blk)
    alpha = jnp.exp2(m - m_new)
    p = jnp.exp2(s - (m_new if shift is None else m_new - shift))
    l_new = alpha * l + jnp.sum(p, axis=-1, keepdims=True)
    pv = _dot_nt(p.astype(BF16), v) if v_transposed else _dot(p.astype(BF16), v)
    acc_new = alpha * acc + pv
    return m_new, l_new, acc_new


def _flash_step(s, vx, m_ref, acc_ref, shift=None):
    tiles = [s[:, t * LANES:(t + 1) * LANES] for t in range(s.shape[1] // LANES)]
    m_blk = jnp.max(functools.reduce(jnp.maximum, tiles), axis=-1, keepdims=True)
    m_prev = m_ref[...]
    m_new = jnp.maximum(m_prev, m_blk if shift is None else m_blk + shift)
    alpha = jnp.exp2(m_prev - m_new)
    sub = m_new if shift is None else m_new - shift
    p = jnp.concatenate([jnp.exp2(t - sub) for t in tiles], axis=1).astype(BF16)
    acc_ref[...] = jnp.concatenate([alpha, alpha], axis=1) * acc_ref[...] + _dot(p, vx)
    m_ref[...] = m_new


def _flash_init(m_ref, acc_ref):
    m_ref[...] = jnp.full_like(m_ref, -jnp.inf)
    acc_ref[...] = jnp.zeros_like(acc_ref)


def _mla_prompt_kernel(q_ref, kt_ref, vx_ref, o_ref, m_ref, acc_ref):
    i = pl.program_id(1)
    tq, tk = MLA_TQ, PROMPT_TK
    per = tq // tk
    rows = MLA_HEADS * tq
    q = q_ref[...].reshape(rows, QPAD)
    _flash_init(m_ref, acc_ref)

    def chunk(j, mask):
        s = _dot(q, kt_ref[j])
        if mask is not None:
            s = jnp.where(mask, s, MASK_VALUE)
        _flash_step(s, vx_ref[pl.ds(pl.multiple_of(j * tk, tk), tk), :], m_ref, acc_ref)

    def body(j, carry):
        chunk(j, None)
        return carry

    lax.fori_loop(0, i * per, body, 0)
    r = lax.broadcasted_iota(jnp.int32, (MLA_HEADS, tq, tk), 1).reshape(rows, tk)
    c = lax.broadcasted_iota(jnp.int32, (rows, tk), 1)
    for d in range(per):
        chunk(i * per + d, c + d * tk <= r)

    for hd in range(MLA_HEADS):
        a = acc_ref[hd * tq:(hd + 1) * tq, :]
        o_ref[:, hd * LANES:(hd + 1) * LANES] = (a[:, 0:LANES] * (1.0 / a[:, LANES:QPAD])).astype(o_ref.dtype)


def _mla_prompt(q, lat_t, vx, batch, seq):
    tq, tk = MLA_TQ, PROMPT_TK
    assert tq % tk == 0
    nq = seq // tq
    rows = MLA_HEADS * tq
    return pl.pallas_call(
        _mla_prompt_kernel,
        out_shape=jax.ShapeDtypeStruct((batch * seq, MLA_HEADS * MLA_KV_LORA), BF16),
        grid=(batch, nq),
        in_specs=[
            pl.BlockSpec((MLA_HEADS, tq, QPAD), lambda b, i: (0, b * nq + i, 0)),
            pl.BlockSpec((seq // tk, QPAD, tk), lambda b, i: (b, 0, 0)),
            pl.BlockSpec((seq, QPAD), lambda b, i: (b, 0)),
        ],
        out_specs=pl.BlockSpec((tq, MLA_HEADS * MLA_KV_LORA), lambda b, i: (b * nq + i, 0)),
        scratch_shapes=[pltpu.VMEM((rows, LANES), F32), pltpu.VMEM((rows, QPAD), F32)],
        compiler_params=_cparams("parallel", "arbitrary"),
        name="mla_prompt",
    )(q, lat_t, vx)


def _diff_lambda(lq1, lk1, lq2, lk2, lam_init):
    return (jnp.exp(jnp.sum(lq1 * lk1, axis=-1, keepdims=True))
            - jnp.exp(jnp.sum(lq2 * lk2, axis=-1, keepdims=True)) + lam_init)


def _diff_combine(o0, o1, lam, subln_g, lam_init):
    return _rms(o0 - lam * o1, subln_g) * (1.0 - lam_init)


def _diff_prompt_kernel(q_ref, kt_ref, vx_ref, bias_ref, far_ref, lq1_ref, lk1_ref, lq2_ref, lk2_ref,
                        subln_ref, o_ref, m_ref, acc_ref, *, lam_init):
    i = pl.program_id(2)
    tq, tk = PROMPT_TQ, PROMPT_TK
    rows = 4 * tq
    q = q_ref[...].reshape(rows, LANES)
    _flash_init(m_ref, acc_ref)

    def biased(j, kind):
        s = _dot(q, kt_ref[j])
        return (s.reshape(DIFF_GROUP, 2, tq, tk) + bias_ref[:, kind][:, None]).reshape(rows, tk)

    def pair(jj, carry):
        s = jnp.concatenate([_dot(q, kt_ref[2 * jj]), _dot(q, kt_ref[2 * jj + 1])], axis=1)
        _flash_step(s, vx_ref[pl.ds(pl.multiple_of(jj * 2 * tk, 2 * tk), 2 * tk), :], m_ref, acc_ref,
                    far_ref[...])
        return carry

    n_far = jnp.maximum(i - 1, 0)
    lax.fori_loop(0, n_far // 2, pair, 0)

    @pl.when(n_far % 2 == 1)
    def _():
        j = n_far - 1
        _flash_step(_dot(q, kt_ref[j]), vx_ref[pl.ds(pl.multiple_of(j * tk, tk), tk), :], m_ref, acc_ref,
                    far_ref[...])

    @pl.when(i > 0)
    def _():
        s = jnp.concatenate([biased(i - 1, 1), biased(i, 0)], axis=1)
        _flash_step(s, vx_ref[pl.ds(pl.multiple_of((i - 1) * tk, tk), 2 * tk), :], m_ref, acc_ref)

    @pl.when(i == 0)
    def _():
        _flash_step(biased(0, 0), vx_ref[0:tk, :], m_ref, acc_ref)

    lam = _diff_lambda(lq1_ref[...], lk1_ref[...], lq2_ref[...], lk2_ref[...], lam_init)
    acc = acc_ref[...]
    o = acc[:, 0:LANES] * (1.0 / acc[:, LANES:2 * LANES])
    for g in range(DIFF_GROUP):
        o0 = o[(2 * g) * tq:(2 * g + 1) * tq, :]
        o1 = o[(2 * g + 1) * tq:(2 * g + 2) * tq, :]
        o_ref[:, g * LANES:(g + 1) * LANES] = _diff_combine(
            o0, o1, lam, subln_ref[...], lam_init).astype(o_ref.dtype)


def _diff_prompt(qd, kd_t, vdx, bias_p, far_rows, wl, lam_init, batch, seq):
    tq, tk = PROMPT_TQ, PROMPT_TK
    nq = seq // tq
    rows = 4 * tq
    assert tq == tk and tk >= REL_FAR
    const3 = lambda b, h, i: (0, 0)
    small = lambda a: pl.BlockSpec(a.shape, const3)
    return pl.pallas_call(
        functools.partial(_diff_prompt_kernel, lam_init=lam_init),
        out_shape=jax.ShapeDtypeStruct((batch * seq, DIFF_HEADS * 2 * DIFF_DH), BF16),
        grid=(batch, DIFF_KV_HEADS, nq),
        in_specs=[
            pl.BlockSpec((4, tq, LANES), lambda b, h, i: (h, b * nq + i, 0)),
            pl.BlockSpec((None, seq // tk, LANES, tk), lambda b, h, i: (h, b, 0, 0)),
            pl.BlockSpec((seq, 2 * LANES), lambda b, h, i: (b, h)),
            pl.BlockSpec((DIFF_GROUP, 2, tq, tk), lambda b, h, i: (h, 0, 0, 0)),
            pl.BlockSpec((None, rows, LANES), lambda b, h, i: (h, 0, 0)),
            small(wl['lq1']), small(wl['lk1']), small(wl['lq2']), small(wl['lk2']), small(wl['subln_g']),
        ],
        out_specs=pl.BlockSpec((tq, DIFF_GROUP * LANES), lambda b, h, i: (b * nq + i, h)),
        scratch_shapes=[pltpu.VMEM((rows, LANES), F32), pltpu.VMEM((rows, 2 * LANES), F32)],
        compiler_params=_cparams("parallel", "parallel", "arbitrary"),
        name="diff_prompt",
    )(qd, kd_t, vdx, bias_p, far_rows, wl['lq1'], wl['lk1'], wl['lq2'], wl['lk2'], wl['subln_g'])


def _decode_kernel(pt_ref, qm_ref, qd_ref, rows_ref, kv_ref, bias_ref, biasn_ref,
                   lq1_ref, lk1_ref, lq2_ref, lk2_ref, subln_ref, cm_hbm, cd_hbm,
                   om_ref, od_ref, mbuf, dbuf, nlat, nkv, sem,
                   *, lam_init, layer, n_chunks, dec_t):
    b = pl.program_id(0)
    nb = pl.num_programs(0)
    mrows = MLA_HEADS * dec_t
    drows = 4 * dec_t
    prow = PAGE_SIZE * 4

    def copies(g):
        bb, c, slot = g // n_chunks, g % n_chunks, g % DEC_SLOTS
        out = []
        for p in range(DEC_PAGES):
            phys = pt_ref[bb, c * DEC_PAGES + p]
            out.append(pltpu.make_async_copy(cm_hbm.at[layer, phys],
                                             mbuf.at[slot, :, pl.ds(p * PAGE_SIZE, PAGE_SIZE)], sem.at[0, slot]))
            out.append(pltpu.make_async_copy(cd_hbm.at[layer, phys], dbuf.at[slot, pl.ds(p * prow, prow)],
                                             sem.at[1, slot]))
        return out

    def start(g):
        for n, cp in enumerate(copies(g)):
            cp.start(priority=(n // 2) % 2)

    @pl.when(b == 0)
    def _():
        nlat[...] = jnp.zeros_like(nlat)
        nkv[...] = jnp.zeros_like(nkv)
        for g0 in range(DEC_SLOTS - 1):
            start(g0)

    qm = qm_ref[...].reshape(mrows, QPAD).astype(BF16)
    qd = [qd_ref[kvh * 4:(kvh + 1) * 4].reshape(drows, LANES).astype(BF16) for kvh in range(DIFF_KV_HEADS)]

    def init(r, d):
        return (jnp.full((r, 1), -jnp.inf, F32), jnp.zeros((r, 1), F32), jnp.zeros((r, d), F32))

    def body(c, carry):
        st_m, st_d = carry
        g = b * n_chunks + c
        slot = g % DEC_SLOTS

        @pl.when(g + DEC_SLOTS - 1 < nb * n_chunks)
        def _():
            start(g + DEC_SLOTS - 1)

        for cp in copies(g):
            cp.wait()

        kt = mbuf[slot].astype(BF16)
        st_m = _online_step(_dot(qm[:, 0:MLA_LAT], kt), kt[0:MLA_KV_LORA, :], *st_m, v_transposed=True)

        kind = lax.convert_element_type(c == n_chunks - 1, jnp.int32)
        new_d = []
        for kvh in range(DIFF_KV_HEADS):
            kd = dbuf[slot, pl.ds(kvh, DEC_TK, stride=4), :].astype(BF16)
            vd = dbuf[slot, pl.ds(2 + kvh, DEC_TK, stride=4), :].astype(BF16)
            s = _dot_nt(qd[kvh], kd) + bias_ref[kind, kvh]
            new_d.append(_online_step(s, vd, *st_d[kvh]))
        return st_m, tuple(new_d)

    st_m, st_d = lax.fori_loop(
        0, n_chunks, body,
        (init(mrows, MLA_KV_LORA), tuple(init(drows, LANES) for _ in range(DIFF_KV_HEADS))))

    nlat[0:dec_t, 0:MLA_KV_LORA] = rows_ref[:, 0:MLA_KV_LORA]
    nlat[0:dec_t, MLA_KV_LORA:MLA_LAT] = rows_ref[:, MLA_KV_LORA:MLA_LAT]
    for j in range(4):
        nkv[0:dec_t, j * LANES:(j + 1) * LANES] = kv_ref[pl.ds(j, dec_t, stride=4), :]
    new_k = nlat[...].astype(BF16)
    rr = lax.broadcasted_iota(jnp.int32, (MLA_HEADS, dec_t, LANES), 1).reshape(mrows, LANES)
    cc = lax.broadcasted_iota(jnp.int32, (mrows, LANES), 1)
    s = jnp.where(cc <= rr, _dot_nt(qm, new_k), MASK_VALUE)
    _, l_m, acc_m = _online_step(s, new_k[:, 0:MLA_KV_LORA], *st_m)
    o_m = acc_m * (1.0 / l_m)
    for hd in range(MLA_HEADS):
        om_ref[:, hd * LANES:(hd + 1) * LANES] = o_m[hd * dec_t:(hd + 1) * dec_t, :]

    lam = _diff_lambda(lq1_ref[...], lk1_ref[...], lq2_ref[...], lk2_ref[...], lam_init)
    for kvh in range(DIFF_KV_HEADS):
        kd = nkv[:, kvh * LANES:(kvh + 1) * LANES].astype(BF16)
        vd = nkv[:, (2 + kvh) * LANES:(3 + kvh) * LANES].astype(BF16)
        s = _dot_nt(qd[kvh], kd) + biasn_ref[kvh]
        _, l_d, acc_d = _online_step(s, vd, *st_d[kvh])
        o = acc_d * (1.0 / l_d)
        for g in range(DIFF_GROUP):
            o0 = o[(2 * g) * dec_t:(2 * g + 1) * dec_t, :]
            o1 = o[(2 * g + 1) * dec_t:(2 * g + 2) * dec_t, :]
            od_ref[:, (kvh * 2 + g) * LANES:(kvh * 2 + g + 1) * LANES] = _diff_combine(
                o0, o1, lam, subln_ref[...], lam_init)


def _decode(page_table, qm, qd, rows, kv, bias_d, bias_n, wl, cache_mla, cache_diff,
            lam_init, layer, dec_b, dec_t):
    n_pages = page_table.shape[1]
    assert n_pages % DEC_PAGES == 0 and dec_b * (n_pages // DEC_PAGES) >= DEC_SLOTS
    n_chunks = n_pages // DEC_PAGES
    n_pool = cache_diff.shape[1]
    cd = cache_diff.reshape(cache_diff.shape[0], n_pool, PAGE_SIZE * 4, LANES)
    cm = jnp.swapaxes(cache_mla, 2, 3)
    const = lambda b, pt: (0, 0)
    small = lambda a: pl.BlockSpec(a.shape, const)
    grid_spec = pltpu.PrefetchScalarGridSpec(
        num_scalar_prefetch=1,
        grid=(dec_b,),
        in_specs=[
            pl.BlockSpec((MLA_HEADS, dec_t, QPAD), lambda b, pt: (0, b, 0)),
            pl.BlockSpec((8, dec_t, LANES), lambda b, pt: (0, b, 0)),
            pl.BlockSpec((dec_t, MLA_LAT), lambda b, pt: (b, 0)),
            pl.BlockSpec((4 * dec_t, LANES), lambda b, pt: (b, 0)),
            pl.BlockSpec(bias_d.shape, lambda b, pt: (0, 0, 0, 0)),
            pl.BlockSpec(bias_n.shape, lambda b, pt: (0, 0, 0)),
            small(wl['lq1']), small(wl['lk1']), small(wl['lq2']), small(wl['lk2']), small(wl['subln_g']),
            pl.BlockSpec(memory_space=pl.ANY),
            pl.BlockSpec(memory_space=pl.ANY),
        ],
        out_specs=(
            pl.BlockSpec((dec_t, MLA_HEADS * MLA_KV_LORA), lambda b, pt: (b, 0)),
            pl.BlockSpec((dec_t, DIFF_HEADS * LANES), lambda b, pt: (b, 0)),
        ),
        scratch_shapes=[
            pltpu.VMEM((DEC_SLOTS, MLA_LAT, DEC_TK), F32),
            pltpu.VMEM((DEC_SLOTS, DEC_TK * 4, LANES), F32),
            pltpu.VMEM((LANES, QPAD), F32),
            pltpu.VMEM((LANES, 4 * LANES), F32),
            pltpu.SemaphoreType.DMA((2, DEC_SLOTS)),
        ],
    )
    return pl.pallas_call(
        functools.partial(_decode_kernel, lam_init=lam_init, layer=layer, n_chunks=n_chunks, dec_t=dec_t),
        out_shape=(
            jax.ShapeDtypeStruct((dec_b * dec_t, MLA_HEADS * MLA_KV_LORA), F32),
            jax.ShapeDtypeStruct((dec_b * dec_t, DIFF_HEADS * LANES), F32),
        ),
        grid_spec=grid_spec,
        compiler_params=_cparams("arbitrary"),
        name="paged_decode",
    )(page_table, qm, qd, rows, kv, bias_d, bias_n,
      wl['lq1'], wl['lk1'], wl['lq2'], wl['lk2'], wl['subln_g'], cm, cd)


def _merge_kernel(x_ref, om_ref, od_ref, pre_g_ref, wg_ref, wuv_ref, woa_ref, wob_ref, wout_ref,
                  post_g_ref, o_ref):
    x = x_ref[...]
    h = _rms(x, pre_g_ref[...]).astype(BF16)
    gates = _dot(h, wg_ref[...])
    v_a = _dot(om_ref[...].astype(BF16), wuv_ref[...])
    y_a = _dot(v_a.astype(BF16), woa_ref[...])
    y_b = _dot(od_ref[...].astype(BF16), wob_ref[...])
    z = jax.nn.sigmoid(gates[:, 0:D_MODEL]) * y_a + jax.nn.sigmoid(gates[:, D_MODEL:2 * D_MODEL]) * y_b
    y = _dot(z.astype(BF16), wout_ref[...])
    o_ref[...] = x + _rms(y, post_g_ref[...])


def _merge(x, o_m, o_d, wl, tm):
    n = x.shape[0]
    const = lambda i: (0, 0)
    w = lambda a: pl.BlockSpec(a.shape, const)
    row = lambda width: pl.BlockSpec((tm, width), lambda i: (i, 0))
    return pl.pallas_call(
        _merge_kernel,
        out_shape=jax.ShapeDtypeStruct((n, D_MODEL), F32),
        grid=(n // tm,),
        in_specs=[row(D_MODEL), row(o_m.shape[1]), row(o_d.shape[1]),
                  w(wl['mix_pre_g']), w(wl['w_gates']), w(wl['w_uv_bd']), w(wl['w_o_mla']),
                  w(wl['w_o_diff']), w(wl['w_out']), w(wl['mix_post_g'])],
        out_specs=row(D_MODEL),
        compiler_params=_cparams("parallel"),
        name="merge",
    )(x, o_m, o_d, wl['mix_pre_g'], wl['w_gates'], wl['w_uv_bd'], wl['w_o_mla'], wl['w_o_diff'],
      wl['w_out'], wl['mix_post_g'])


def _rope_tables(pos, n_rows):
    half = MLA_ROPE // 2
    inv_freq = ROPE_THETA ** (-jnp.arange(half, dtype=F32) / half)
    ang = pos.astype(F32)[:, None] * inv_freq[None, :]
    cos, sin = jnp.cos(ang), jnp.sin(ang)
    zeros = jnp.zeros((pos.shape[0], LANES - MLA_ROPE), F32)
    cos_t = jnp.concatenate([cos, cos, zeros], axis=1)
    sin_t = jnp.concatenate([-sin, sin, zeros], axis=1)
    reps = n_rows // pos.shape[0]
    return jnp.tile(cos_t, (reps, 1)), jnp.tile(sin_t, (reps, 1))


def _layer_weights(layer, p):
    g = lambda name: p[name][layer].reshape(1, -1)
    w_in = p['w_in'][layer]
    o = np.cumsum((MLA_Q_LORA, MLA_KV_LORA, MLA_ROPE, DIFF_HEADS * 2 * DIFF_DH,
                   DIFF_KV_HEADS * 2 * DIFF_DH, DIFF_KV_HEADS * 2 * DIFF_DH)).tolist()
    half = MLA_ROPE // 2
    kr = w_in[:, o[1]:o[2]]
    zpad = jnp.zeros((D_MODEL, LANES - MLA_ROPE), F32)
    kr_a = jnp.concatenate([kr, zpad], axis=1)
    kr_b = jnp.concatenate([kr[:, half:], kr[:, :half], zpad], axis=1)
    w_in_a = jnp.concatenate([w_in[:, :o[1]], kr_a, kr_b, w_in[:, o[2]:o[5]]], axis=1).astype(BF16)

    w_uq = p['w_uq'][layer].reshape(MLA_Q_LORA, MLA_HEADS, MLA_NOPE + MLA_ROPE)
    wq_nope = jnp.moveaxis(w_uq[:, :, :MLA_NOPE], 1, 0).astype(BF16)
    wk_t = jnp.transpose(p['w_uk'][layer], (1, 2, 0)).astype(BF16)
    x1 = w_uq[:, :, MLA_NOPE:MLA_NOPE + half]
    x2 = w_uq[:, :, MLA_NOPE + half:]
    hz = jnp.zeros((MLA_Q_LORA, MLA_HEADS, LANES - MLA_ROPE), F32)
    w_rope_a = jnp.concatenate([x1, x2, hz], axis=2).reshape(MLA_Q_LORA, MLA_HEADS * LANES).astype(BF16)
    w_rope_b = jnp.concatenate([x2, x1, hz], axis=2).reshape(MLA_Q_LORA, MLA_HEADS * LANES).astype(BF16)

    eye = jnp.eye(MLA_HEADS, dtype=F32)
    w_uv_bd = jnp.einsum('lhv,hg->hlgv', p['w_uv'][layer], eye).reshape(
        MLA_HEADS * MLA_KV_LORA, MLA_HEADS * MLA_V).astype(BF16)
    return {
        'ffn1_pre_g': g('ffn1_pre_g'), 'ffn1_w_gu': p['ffn1_w_gu'][layer].astype(BF16),
        'ffn1_w_down': p['ffn1_w_down'][layer].astype(BF16), 'ffn1_post_g': g('ffn1_post_g'),
        'ffn2_pre_g': g('ffn2_pre_g'), 'ffn2_w_gu': p['ffn2_w_gu'][layer].astype(BF16),
        'ffn2_w_down': p['ffn2_w_down'][layer].astype(BF16), 'ffn2_post_g': g('ffn2_post_g'),
        'mix_pre_g': g('mix_pre_g'), 'w_in_a': w_in_a, 'w_gates': w_in[:, o[5]:].astype(BF16),
        'q_norm_g': g('mla_q_norm_g'), 'kv_norm_g': g('mla_kv_norm_g'),
        'w_abs': _absorb(wq_nope, wk_t), 'w_rope_a': w_rope_a, 'w_rope_b': w_rope_b,
        'w_uv_bd': w_uv_bd, 'w_o_mla': p['w_o_mla'][layer].astype(BF16),
        'w_o_diff': p['w_o_diff'][layer].astype(BF16), 'w_out': p['w_out'][layer].astype(BF16),
        'mix_post_g': g('mix_post_g'),
        'lq1': g('diff_lq1'), 'lk1': g('diff_lk1'), 'lq2': g('diff_lq2'), 'lk2': g('diff_lk2'),
        'subln_g': g('diff_subln_g'),
    }


def _far_rows(rel_table, rows_per_map):
    far = (rel_table[REL_BUCKETS - 1].astype(F32) * LOG2E).reshape(DIFF_KV_HEADS, DIFF_GROUP, 1, 1)
    return jnp.broadcast_to(far, (DIFF_KV_HEADS, DIFF_GROUP, 2 * rows_per_map, LANES)).reshape(
        DIFF_KV_HEADS, 4 * rows_per_map, LANES)


def kernel(x_prompt, x_sample, cache_mla, cache_diff, page_table, ffn1_pre_g, ffn1_w_gu, ffn1_w_down, ffn1_post_g, mix_pre_g, w_in, mla_q_norm_g, w_uq, mla_kv_norm_g, w_uk, w_uv, diff_lq1, diff_lk1, diff_lq2, diff_lk2, diff_subln_g, rel_table, w_o_mla, w_o_diff, w_out, mix_post_g, ffn2_pre_g, ffn2_w_gu, ffn2_w_down, ffn2_post_g):
    params = dict(
        ffn1_pre_g=ffn1_pre_g, ffn1_w_gu=ffn1_w_gu, ffn1_w_down=ffn1_w_down, ffn1_post_g=ffn1_post_g,
        mix_pre_g=mix_pre_g, w_in=w_in, mla_q_norm_g=mla_q_norm_g, w_uq=w_uq,
        mla_kv_norm_g=mla_kv_norm_g, w_uk=w_uk, w_uv=w_uv, diff_lq1=diff_lq1, diff_lk1=diff_lk1,
        diff_lq2=diff_lq2, diff_lk2=diff_lk2, diff_subln_g=diff_subln_g, w_o_mla=w_o_mla,
        w_o_diff=w_o_diff, w_out=w_out, mix_post_g=mix_post_g, ffn2_pre_g=ffn2_pre_g,
        ffn2_w_gu=ffn2_w_gu, ffn2_w_down=ffn2_w_down, ffn2_post_g=ffn2_post_g)
    batch, seq, _ = x_prompt.shape
    dec_b, dec_t, _ = x_sample.shape
    depth = cache_mla.shape[0]
    past_len = page_table.shape[1] * PAGE_SIZE
    n_p, n_s = batch * seq, dec_b * dec_t
    tm_p, tm_s = 512, 512

    cos_p, sin_p = _rope_tables(jnp.arange(seq), seq)
    cos_s, sin_s = _rope_tables(past_len + jnp.arange(dec_t), tm_s)
    bias_p, bias_d, bias_n = _rel_bias_tiles(rel_table.astype(F32), past_len, dec_t)
    far_p = _far_rows(rel_table, PROMPT_TQ)

    y_p = x_prompt.reshape(n_p, D_MODEL)
    y_s = x_sample.reshape(n_s, D_MODEL)
    outs = ([], [], [], [])
    for layer in range(depth):
        wl = _layer_weights(layer, params)
        lam_init = 0.8 - 0.6 * math.exp(-0.3 * layer)

        y_p = _ffn(y_p, wl['ffn1_pre_g'], wl['ffn1_w_gu'], wl['ffn1_w_down'], wl['ffn1_post_g'], FFN_TM)
        q, lat_t, vx, rows_p, qd, kd_t, vdx, kv_p = _mix_in(y_p, wl, cos_p, sin_p, tm_p, BF16)
        o_m = _mla_prompt(q, lat_t, vx, batch, seq)
        o_d = _diff_prompt(qd, kd_t, vdx, bias_p, far_p, wl, lam_init, batch, seq)
        y_p = _merge(y_p, o_m, o_d, wl, tm_p)
        y_p = _ffn(y_p, wl['ffn2_pre_g'], wl['ffn2_w_gu'], wl['ffn2_w_down'], wl['ffn2_post_g'], FFN_TM)

        y_s = _ffn(y_s, wl['ffn1_pre_g'], wl['ffn1_w_gu'], wl['ffn1_w_down'], wl['ffn1_post_g'], FFN_TM)
        q, _, _, rows_s, qd, _, _, kv_s = _mix_in(y_s, wl, cos_s, sin_s, tm_s, F32)
        o_m, o_d = _decode(page_table, q, qd, rows_s, kv_s, bias_d, bias_n, wl,
                           cache_mla, cache_diff, lam_init, layer, dec_b, dec_t)
        y_s = _merge(y_s, o_m, o_d, wl, tm_s)
        y_s = _ffn(y_s, wl['ffn2_pre_g'], wl['ffn2_w_gu'], wl['ffn2_w_down'], wl['ffn2_post_g'], FFN_TM)

        outs[0].append(rows_p.reshape(batch, seq, MLA_LAT))
        outs[1].append(kv_p.reshape(batch, seq, 2, DIFF_KV_HEADS, 2 * DIFF_DH))
        outs[2].append(rows_s.reshape(dec_b, dec_t, MLA_LAT))
        outs[3].append(kv_s.reshape(dec_b, dec_t, 2, DIFF_KV_HEADS, 2 * DIFF_DH))
    return (y_p.reshape(batch, seq, D_MODEL), y_s.reshape(dec_b, dec_t, D_MODEL),
            jnp.stack(outs[0]), jnp.stack(outs[1]), jnp.stack(outs[2]), jnp.stack(outs[3]))
```

```python
import functools
import math

import numpy as np
import jax
import jax.numpy as jnp
from jax import lax
from jax.experimental import pallas as pl
from jax.experimental.pallas import tpu as pltpu

F32 = jnp.float32
BF16 = jnp.bfloat16

D_MODEL = 1024
MLA_HEADS = 8
MLA_NOPE = 64
MLA_ROPE = 32
MLA_V = 64
MLA_Q_LORA = D_MODEL // 4
MLA_KV_LORA = D_MODEL // 8
MLA_LAT = MLA_KV_LORA + MLA_ROPE
MLA_SCALE = (MLA_NOPE + MLA_ROPE) ** -0.5
DIFF_HEADS = 4
DIFF_KV_HEADS = 2
DIFF_GROUP = DIFF_HEADS // DIFF_KV_HEADS
DIFF_DH = 64
DIFF_SCALE = DIFF_DH ** -0.5
REL_BUCKETS = 32
REL_MAX_DIST = 128
D_FF = ((8 * D_MODEL // 3 + 127) // 128) * 128
ROPE_THETA = 10000.0
NORM_EPS = 1e-6
MASK_VALUE = -1e30
PAGE_SIZE = 128
LOG2E = math.log2(math.e)
MLA_QSCALE = MLA_SCALE * LOG2E
DIFF_QSCALE = DIFF_SCALE * LOG2E

LANES = 128
QPAD = 2 * LANES
VMEM_LIMIT = 56 * 1024 * 1024

FFN_TF = 256
FFN_TM = 1024
PROMPT_TQ = 256
MLA_TQ = 512
PROMPT_TK = 256
DEC_PAGES = 32
DEC_TK = DEC_PAGES * PAGE_SIZE
DEC_SLOTS = 3


def _bucket_thresholds():
    max_exact = REL_BUCKETS // 2
    n = np.arange(0, 4 * REL_MAX_DIST)
    large = max_exact + (np.log(np.maximum(n, 1).astype(np.float32) / max_exact)
                         / math.log(REL_MAX_DIST / max_exact)
                         * (REL_BUCKETS - max_exact)).astype(np.int32)
    bucket = np.where(n < max_exact, n, np.minimum(large, REL_BUCKETS - 1))
    assert np.all(np.diff(bucket) >= 0)
    return [int(np.argmax(bucket >= k)) for k in range(REL_BUCKETS)]


_BUCKET_THR = _bucket_thresholds()
REL_FAR = _BUCKET_THR[REL_BUCKETS - 1]


def _rms(x, g):
    return x * lax.rsqrt(jnp.mean(x * x, axis=-1, keepdims=True) + NORM_EPS) * g


def _dot(a, b):
    return jnp.dot(a, b, preferred_element_type=F32)


def _dot_nt(a, b):
    return lax.dot_general(a, b, (((1,), (1,)), ((), ())), preferred_element_type=F32)


def _cparams(*sem):
    return pltpu.CompilerParams(dimension_semantics=sem, vmem_limit_bytes=VMEM_LIMIT)


def _ffn_kernel(x_ref, pre_g_ref, wg_ref, wu_ref, wd_ref, post_g_ref, o_ref, h_ref, acc_ref):
    f = pl.program_id(1)

    @pl.when(f == 0)
    def _():
        h_ref[...] = _rms(x_ref[...], pre_g_ref[...]).astype(BF16)
        acc_ref[...] = jnp.zeros_like(acc_ref)

    h = h_ref[...]
    gate = _dot(h, wg_ref[...])
    up = _dot(h, wu_ref[...])
    act = (gate * jax.nn.sigmoid(gate) * up).astype(BF16)
    acc_ref[...] += _dot(act, wd_ref[...])

    @pl.when(f == pl.num_programs(1) - 1)
    def _():
        o_ref[...] = x_ref[...] + 0.5 * _rms(acc_ref[...], post_g_ref[...])


def _ffn(x, pre_g, w_gu, w_down, post_g, tm):
    n = x.shape[0]
    nf = D_FF // FFN_TF
    return pl.pallas_call(
        _ffn_kernel,
        out_shape=jax.ShapeDtypeStruct((n, D_MODEL), F32),
        grid=(n // tm, nf),
        in_specs=[
            pl.BlockSpec((tm, D_MODEL), lambda i, f: (i, 0)),
            pl.BlockSpec((1, D_MODEL), lambda i, f: (0, 0)),
            pl.BlockSpec((D_MODEL, FFN_TF), lambda i, f: (0, f)),
            pl.BlockSpec((D_MODEL, FFN_TF), lambda i, f: (0, f + D_FF // FFN_TF)),
            pl.BlockSpec((FFN_TF, D_MODEL), lambda i, f: (f, 0)),
            pl.BlockSpec((1, D_MODEL), lambda i, f: (0, 0)),
        ],
        out_specs=pl.BlockSpec((tm, D_MODEL), lambda i, f: (i, 0)),
        scratch_shapes=[pltpu.VMEM((tm, D_MODEL), BF16), pltpu.VMEM((tm, D_MODEL), F32)],
        compiler_params=_cparams("parallel", "arbitrary"),
        name="ffn_half",
    )(x, pre_g, w_gu, w_gu, w_down, post_g)


def _absorb_kernel(wq_ref, wk_ref, o_ref):
    for h in range(MLA_HEADS):
        o_ref[:, h * LANES:(h + 1) * LANES] = _dot(wq_ref[h], wk_ref[h]).astype(BF16)


def _absorb(wq_nope, wk_t):
    return pl.pallas_call(
        _absorb_kernel,
        out_shape=jax.ShapeDtypeStruct((MLA_Q_LORA, MLA_HEADS * MLA_KV_LORA), BF16),
        name="absorb_uk",
    )(wq_nope, wk_t)


def _bias_from_dist(dist, table_ref, head):
    val = jnp.full(dist.shape, table_ref[0, head] * LOG2E, F32)
    for k in range(1, REL_BUCKETS):
        val = jnp.where(dist >= _BUCKET_THR[k], table_ref[k, head] * LOG2E, val)
    return jnp.where(dist < 0, MASK_VALUE, val)


def _bias_kernel(table_ref, bp_ref, bd_ref, bn_ref, *, past_len, dec_t):
    r = lax.broadcasted_iota(jnp.int32, (PROMPT_TQ, PROMPT_TK), 0)
    c = lax.broadcasted_iota(jnp.int32, (PROMPT_TQ, PROMPT_TK), 1)
    for head in range(DIFF_HEADS):
        for kind in range(2):
            bp_ref[head, kind] = _bias_from_dist(r - c + kind * PROMPT_TK, table_ref, head)
    t = lax.broadcasted_iota(jnp.int32, (dec_t, DEC_TK), 0)
    c = lax.broadcasted_iota(jnp.int32, (dec_t, DEC_TK), 1)
    tn = lax.broadcasted_iota(jnp.int32, (dec_t, LANES), 0)
    cn = lax.broadcasted_iota(jnp.int32, (dec_t, LANES), 1)
    for kvh in range(DIFF_KV_HEADS):
        for g in range(DIFF_GROUP):
            head = kvh * DIFF_GROUP + g
            far = _bias_from_dist(t - c + past_len, table_ref, head)
            last = _bias_from_dist(t - c + DEC_TK, table_ref, head)
            new = jnp.where(cn < dec_t, _bias_from_dist(tn - cn, table_ref, head), MASK_VALUE)
            for m in range(2):
                row = (g * 2 + m) * dec_t
                bd_ref[0, kvh, row:row + dec_t, :] = far
                bd_ref[1, kvh, row:row + dec_t, :] = last
                bn_ref[kvh, row:row + dec_t, :] = new


def _rel_bias_tiles(rel_table, past_len, dec_t):
    assert past_len - DEC_TK >= REL_FAR and dec_t <= LANES
    return pl.pallas_call(
        functools.partial(_bias_kernel, past_len=past_len, dec_t=dec_t),
        out_shape=(
            jax.ShapeDtypeStruct((DIFF_HEADS, 2, PROMPT_TQ, PROMPT_TK), F32),
            jax.ShapeDtypeStruct((2, DIFF_KV_HEADS, 4 * dec_t, DEC_TK), F32),
            jax.ShapeDtypeStruct((DIFF_KV_HEADS, 4 * dec_t, LANES), F32),
        ),
        in_specs=[pl.BlockSpec(memory_space=pltpu.SMEM)],
        name="rel_bias_tiles",
    )(rel_table)


def _mix_in_kernel(x_ref, pre_g_ref, w_in_ref, qg_ref, w_abs_ref, w_ra_ref, w_rb_ref, kvg_ref,
                   cos_ref, sin_ref,
                   q_ref, lat_t_ref, vx_ref, rows_ref, qd_ref, kd_t_ref, vdx_ref, kv_ref):
    tm = x_ref.shape[0]
    tk = PROMPT_TK
    h = _rms(x_ref[...], pre_g_ref[...]).astype(BF16)
    p = _dot(h, w_in_ref[...])
    cq = p[:, 0:256]
    ckv = p[:, 256:384]
    kr_a = p[:, 384:512]
    kr_b = p[:, 512:640]
    dq = p[:, 640:1152]
    dk = p[:, 1152:1408]
    dv = p[:, 1408:1664]
    cos = cos_ref[...]
    sin = sin_ref[...]

    cqn = _rms(cq, qg_ref[...]).astype(BF16)
    q_lat = _dot(cqn, w_abs_ref[...])
    r_a = _dot(cqn, w_ra_ref[...])
    r_b = _dot(cqn, w_rb_ref[...])
    for hd in range(MLA_HEADS):
        sl = slice(hd * LANES, (hd + 1) * LANES)
        q_ref[hd, :, 0:LANES] = (q_lat[:, sl] * MLA_QSCALE).astype(q_ref.dtype)
        q_ref[hd, :, LANES:QPAD] = ((r_a[:, sl] * cos + r_b[:, sl] * sin) * MLA_QSCALE).astype(q_ref.dtype)

    ckvn = _rms(ckv, kvg_ref[...])
    k_rope = kr_a * cos + kr_b * sin
    rows_ref[:, 0:MLA_KV_LORA] = ckvn
    rows_ref[:, MLA_KV_LORA:MLA_LAT] = k_rope[:, 0:MLA_ROPE]
    for c in range(tm // tk):
        rs = slice(c * tk, (c + 1) * tk)
        lat_t_ref[c, 0:LANES, :] = ckvn[rs].T.astype(BF16)
        lat_t_ref[c, LANES:QPAD, :] = k_rope[rs].T.astype(BF16)
        for kvh in range(DIFF_KV_HEADS):
            kd_t_ref[kvh, c] = dk[rs, kvh * LANES:(kvh + 1) * LANES].T.astype(BF16)
    ones = jnp.ones((tm, LANES), BF16)
    vx_ref[:, 0:LANES] = ckvn.astype(BF16)
    vx_ref[:, LANES:QPAD] = ones
    for kvh in range(DIFF_KV_HEADS):
        vdx_ref[:, 2 * kvh * LANES:(2 * kvh + 1) * LANES] = dv[:, kvh * LANES:(kvh + 1) * LANES].astype(BF16)
        vdx_ref[:, (2 * kvh + 1) * LANES:(2 * kvh + 2) * LANES] = ones

    lane = lax.broadcasted_iota(jnp.int32, (tm, LANES), 1)
    for kvh in range(DIFF_KV_HEADS):
        for g in range(DIFF_GROUP):
            pair = dq[:, (kvh * 2 + g) * LANES:(kvh * 2 + g + 1) * LANES] * DIFF_QSCALE
            for m in range(2):
                keep = (lane < DIFF_DH) if m == 0 else (lane >= DIFF_DH)
                qd_ref[kvh * 4 + g * 2 + m] = jnp.where(keep, pair, 0.0).astype(qd_ref.dtype)
    for j, piece in enumerate((dk[:, 0:LANES], dk[:, LANES:2 * LANES], dv[:, 0:LANES], dv[:, LANES:2 * LANES])):
        kv_ref[pl.ds(j, tm, stride=4), :] = piece


def _mix_in(x, wl, cos_t, sin_t, tm, q_dtype):
    n = x.shape[0]
    n_tab = cos_t.shape[0] // tm
    tk = PROMPT_TK
    const = lambda i: (0, 0)
    w = lambda a: pl.BlockSpec(a.shape, const)
    return pl.pallas_call(
        _mix_in_kernel,
        out_shape=(
            jax.ShapeDtypeStruct((MLA_HEADS, n, QPAD), q_dtype),
            jax.ShapeDtypeStruct((n // tk, QPAD, tk), BF16),
            jax.ShapeDtypeStruct((n, QPAD), BF16),
            jax.ShapeDtypeStruct((n, MLA_LAT), F32),
            jax.ShapeDtypeStruct((8, n, LANES), q_dtype),
            jax.ShapeDtypeStruct((DIFF_KV_HEADS, n // tk, LANES, tk), BF16),
            jax.ShapeDtypeStruct((n, 2 * DIFF_KV_HEADS * LANES), BF16),
            jax.ShapeDtypeStruct((4 * n, LANES), F32),
        ),
        grid=(n // tm,),
        in_specs=[
            pl.BlockSpec((tm, D_MODEL), lambda i: (i, 0)),
            w(wl['mix_pre_g']), w(wl['w_in_a']), w(wl['q_norm_g']), w(wl['w_abs']),
            w(wl['w_rope_a']), w(wl['w_rope_b']), w(wl['kv_norm_g']),
            pl.BlockSpec((tm, LANES), lambda i: (i % n_tab, 0)),
            pl.BlockSpec((tm, LANES), lambda i: (i % n_tab, 0)),
        ],
        out_specs=(
            pl.BlockSpec((MLA_HEADS, tm, QPAD), lambda i: (0, i, 0)),
            pl.BlockSpec((tm // tk, QPAD, tk), lambda i: (i, 0, 0)),
            pl.BlockSpec((tm, QPAD), lambda i: (i, 0)),
            pl.BlockSpec((tm, MLA_LAT), lambda i: (i, 0)),
            pl.BlockSpec((8, tm, LANES), lambda i: (0, i, 0)),
            pl.BlockSpec((DIFF_KV_HEADS, tm // tk, LANES, tk), lambda i: (0, i, 0, 0)),
            pl.BlockSpec((tm, 2 * DIFF_KV_HEADS * LANES), lambda i: (i, 0)),
            pl.BlockSpec((4 * tm, LANES), lambda i: (i, 0)),
        ),
        compiler_params=_cparams("parallel"),
        name="mix_in",
    )(x, wl['mix_pre_g'], wl['w_in_a'], wl['q_norm_g'], wl['w_abs'], wl['w_rope_a'], wl['w_rope_b'],
      wl['kv_norm_g'], cos_t, sin_t)


def _online_step(s, v, m, l, acc, shift=None, v_transposed=False):
    m_blk = jnp.max(s, axis=-1, keepdims=True)
    if shift is not None:
        m_blk = m_blk + shift
    m_new = jnp.maximum(m, m_blk)
    alpha = jnp.exp2(m - m_new)
    p = jnp.exp2(s - (m_new if shift is None else m_new - shift))
    l_new = alpha * l + jnp.sum(p, axis=-1, keepdims=True)
    pv = _dot_nt(p.astype(BF16), v) if v_transposed else _dot(p.astype(BF16), v)
    acc_new = alpha * acc + pv
    return m_new, l_new, acc_new


def _flash_step(s, vx, m_ref, acc_ref, shift=None):
    tiles = [s[:, t * LANES:(t + 1) * LANES] for t in range(s.shape[1] // LANES)]
    m_blk = jnp.max(functools.reduce(jnp.maximum, tiles), axis=-1, keepdims=True)
    m_prev = m_ref[...].reshape(-1, LANES)
    m_new = jnp.maximum(m_prev, m_blk if shift is None else m_blk + shift)
    alpha = jnp.exp2(m_prev - m_new)
    sub = m_new if shift is None else m_new - shift
    p = jnp.concatenate([jnp.exp2(t - sub) for t in tiles], axis=1).astype(BF16)
    acc = jnp.concatenate([alpha, alpha], axis=1) * acc_ref[...].reshape(-1, 2 * LANES) + _dot(p, vx)
    acc_ref[...] = acc.reshape(acc_ref.shape)
    m_ref[...] = m_new.reshape(m_ref.shape)


def _flash_init(m_ref, acc_ref):
    m_ref[...] = jnp.full_like(m_ref, -jnp.inf)
    acc_ref[...] = jnp.zeros_like(acc_ref)


def _mla_prompt_kernel(q_ref, kt_ref, vx_ref, o_ref, m_ref, acc_ref):
    i = pl.program_id(1)
    tq, tk = MLA_TQ, PROMPT_TK
    per = tq // tk
    rows = MLA_HEADS * tq
    q = q_ref[...].reshape(rows, QPAD)
    _flash_init(m_ref, acc_ref)

    def values(j):
        return vx_ref[pl.ds(pl.multiple_of(j * tk, tk), tk), :]

    def body(j, carry):
        _flash_step(_dot(q, kt_ref[j]), values(j), m_ref, acc_ref)
        return carry

    lax.fori_loop(0, i * per, body, 0)
    for d in range(per):
        lo = d * tk
        sub = tq - lo
        qd = q_ref[:, lo:tq, :].reshape(MLA_HEADS * sub, QPAD)
        r = lax.broadcasted_iota(jnp.int32, (MLA_HEADS, sub, tk), 1).reshape(MLA_HEADS * sub, tk)
        c = lax.broadcasted_iota(jnp.int32, (MLA_HEADS * sub, tk), 1)
        s = jnp.where(c <= r, _dot(qd, kt_ref[i * per + d]), MASK_VALUE)
        _flash_step(s, values(i * per + d), m_ref.at[:, pl.ds(lo, sub), :], acc_ref.at[:, pl.ds(lo, sub), :])

    for hd in range(MLA_HEADS):
        a = acc_ref[hd]
        o_ref[:, hd * LANES:(hd + 1) * LANES] = (a[:, 0:LANES] * (1.0 / a[:, LANES:QPAD])).astype(o_ref.dtype)


def _mla_prompt(q, lat_t, vx, batch, seq):
    tq, tk = MLA_TQ, PROMPT_TK
    assert tq % tk == 0
    nq = seq // tq
    rows = MLA_HEADS * tq
    return pl.pallas_call(
        _mla_prompt_kernel,
        out_shape=jax.ShapeDtypeStruct((batch * seq, MLA_HEADS * MLA_KV_LORA), BF16),
        grid=(batch, nq),
        in_specs=[
            pl.BlockSpec((MLA_HEADS, tq, QPAD), lambda b, i: (0, b * nq + i, 0)),
            pl.BlockSpec((seq // tk, QPAD, tk), lambda b, i: (b, 0, 0)),
            pl.BlockSpec((seq, QPAD), lambda b, i: (b, 0)),
        ],
        out_specs=pl.BlockSpec((tq, MLA_HEADS * MLA_KV_LORA), lambda b, i: (b * nq + i, 0)),
        scratch_shapes=[pltpu.VMEM((MLA_HEADS, tq, LANES), F32), pltpu.VMEM((MLA_HEADS, tq, QPAD), F32)],
        compiler_params=_cparams("parallel", "arbitrary"),
        name="mla_prompt",
    )(q, lat_t, vx)


def _diff_lambda(lq1, lk1, lq2, lk2, lam_init):
    return (jnp.exp(jnp.sum(lq1 * lk1, axis=-1, keepdims=True))
            - jnp.exp(jnp.sum(lq2 * lk2, axis=-1, keepdims=True)) + lam_init)


def _diff_combine(o0, o1, lam, subln_g, lam_init):
    return _rms(o0 - lam * o1, subln_g) * (1.0 - lam_init)


def _diff_prompt_kernel(q_ref, kt_ref, vx_ref, bias_ref, far_ref, lq1_ref, lk1_ref, lq2_ref, lk2_ref,
                        subln_ref, o_ref, m_ref, acc_ref, *, lam_init):
    i = pl.program_id(2)
    tq, tk = PROMPT_TQ, PROMPT_TK
    rows = 4 * tq
    q = q_ref[...].reshape(rows, LANES)
    _flash_init(m_ref, acc_ref)

    def biased(j, kind):
        s = _dot(q, kt_ref[j])
        return (s.reshape(DIFF_GROUP, 2, tq, tk) + bias_ref[:, kind][:, None]).reshape(rows, tk)

    def pair(jj, carry):
        s = jnp.concatenate([_dot(q, kt_ref[2 * jj]), _dot(q, kt_ref[2 * jj + 1])], axis=1)
        _flash_step(s, vx_ref[pl.ds(pl.multiple_of(jj * 2 * tk, 2 * tk), 2 * tk), :], m_ref, acc_ref,
                    far_ref[...])
        return carry

    n_far = jnp.maximum(i - 1, 0)
    lax.fori_loop(0, n_far // 2, pair, 0)

    @pl.when(n_far % 2 == 1)
    def _():
        j = n_far - 1
        _flash_step(_dot(q, kt_ref[j]), vx_ref[pl.ds(pl.multiple_of(j * tk, tk), tk), :], m_ref, acc_ref,
                    far_ref[...])

    @pl.when(i > 0)
    def _():
        s = jnp.concatenate([biased(i - 1, 1), biased(i, 0)], axis=1)
        _flash_step(s, vx_ref[pl.ds(pl.multiple_of((i - 1) * tk, tk), 2 * tk), :], m_ref, acc_ref)

    @pl.when(i == 0)
    def _():
        _flash_step(biased(0, 0), vx_ref[0:tk, :], m_ref, acc_ref)

    lam = _diff_lambda(lq1_ref[...], lk1_ref[...], lq2_ref[...], lk2_ref[...], lam_init)
    acc = acc_ref[...]
    o = acc[:, 0:LANES] * (1.0 / acc[:, LANES:2 * LANES])
    for g in range(DIFF_GROUP):
        o0 = o[(2 * g) * tq:(2 * g + 1) * tq, :]
        o1 = o[(2 * g + 1) * tq:(2 * g + 2) * tq, :]
        o_ref[:, g * LANES:(g + 1) * LANES] = _diff_combine(
            o0, o1, lam, subln_ref[...], lam_init).astype(o_ref.dtype)


def _diff_prompt(qd, kd_t, vdx, bias_p, far_rows, wl, lam_init, batch, seq):
    tq, tk = PROMPT_TQ, PROMPT_TK
    nq = seq // tq
    rows = 4 * tq
    assert tq == tk and tk >= REL_FAR
    const3 = lambda b, h, i: (0, 0)
    small = lambda a: pl.BlockSpec(a.shape, const3)
    return pl.pallas_call(
        functools.partial(_diff_prompt_kernel, lam_init=lam_init),
        out_shape=jax.ShapeDtypeStruct((batch * seq, DIFF_HEADS * 2 * DIFF_DH), BF16),
        grid=(batch, DIFF_KV_HEADS, nq),
        in_specs=[
            pl.BlockSpec((4, tq, LANES), lambda b, h, i: (h, b * nq + i, 0)),
            pl.BlockSpec((None, seq // tk, LANES, tk), lambda b, h, i: (h, b, 0, 0)),
            pl.BlockSpec((seq, 2 * LANES), lambda b, h, i: (b, h)),
            pl.BlockSpec((DIFF_GROUP, 2, tq, tk), lambda b, h, i: (h, 0, 0, 0)),
            pl.BlockSpec((None, rows, LANES), lambda b, h, i: (h, 0, 0)),
            small(wl['lq1']), small(wl['lk1']), small(wl['lq2']), small(wl['lk2']), small(wl['subln_g']),
        ],
        out_specs=pl.BlockSpec((tq, DIFF_GROUP * LANES), lambda b, h, i: (b * nq + i, h)),
        scratch_shapes=[pltpu.VMEM((rows, LANES), F32), pltpu.VMEM((rows, 2 * LANES), F32)],
        compiler_params=_cparams("parallel", "parallel", "arbitrary"),
        name="diff_prompt",
    )(qd, kd_t, vdx, bias_p, far_rows, wl['lq1'], wl['lk1'], wl['lq2'], wl['lk2'], wl['subln_g'])


def _decode_kernel(pt_ref, qm_ref, qd_ref, rows_ref, kv_ref, bias_ref, biasn_ref,
                   lq1_ref, lk1_ref, lq2_ref, lk2_ref, subln_ref, cm_hbm, cd_hbm,
                   om_ref, od_ref, mbuf, dbuf, nlat, nkv, sem,
                   *, lam_init, layer, n_chunks, dec_t):
    b = pl.program_id(0)
    nb = pl.num_programs(0)
    mrows = MLA_HEADS * dec_t
    drows = 4 * dec_t
    prow = PAGE_SIZE * 4

    def copies(g):
        bb, c, slot = g // n_chunks, g % n_chunks, g % DEC_SLOTS
        out = []
        for p in range(DEC_PAGES):
            phys = pt_ref[bb, c * DEC_PAGES + p]
            out.append(pltpu.make_async_copy(cm_hbm.at[layer, phys],
                                             mbuf.at[slot, :, pl.ds(p * PAGE_SIZE, PAGE_SIZE)], sem.at[0, slot]))
            out.append(pltpu.make_async_copy(cd_hbm.at[layer, phys], dbuf.at[slot, pl.ds(p * prow, prow)],
                                             sem.at[1, slot]))
        return out

    def start(g):
        for n, cp in enumerate(copies(g)):
            cp.start(priority=(n // 2) % 2)

    @pl.when(b == 0)
    def _():
        nlat[...] = jnp.zeros_like(nlat)
        nkv[...] = jnp.zeros_like(nkv)
        for g0 in range(DEC_SLOTS - 1):
            start(g0)

    qm = qm_ref[...].reshape(mrows, QPAD).astype(BF16)
    qd = [qd_ref[kvh * 4:(kvh + 1) * 4].reshape(drows, LANES).astype(BF16) for kvh in range(DIFF_KV_HEADS)]

    def init(r, d):
        return (jnp.full((r, 1), -jnp.inf, F32), jnp.zeros((r, 1), F32), jnp.zeros((r, d), F32))

    def body(c, carry):
        st_m, st_d = carry
        g = b * n_chunks + c
        slot = g % DEC_SLOTS

        @pl.when(g + DEC_SLOTS - 1 < nb * n_chunks)
        def _():
            start(g + DEC_SLOTS - 1)

        for cp in copies(g):
            cp.wait()

        kt = mbuf[slot].astype(BF16)
        st_m = _online_step(_dot(qm[:, 0:MLA_LAT], kt), kt[0:MLA_KV_LORA, :], *st_m, v_transposed=True)

        kind = lax.convert_element_type(c == n_chunks - 1, jnp.int32)
        new_d = []
        for kvh in range(DIFF_KV_HEADS):
            kd = dbuf[slot, pl.ds(kvh, DEC_TK, stride=4), :].astype(BF16)
            vd = dbuf[slot, pl.ds(2 + kvh, DEC_TK, stride=4), :].astype(BF16)
            s = _dot_nt(qd[kvh], kd) + bias_ref[kind, kvh]
            new_d.append(_online_step(s, vd, *st_d[kvh]))
        return st_m, tuple(new_d)

    st_m, st_d = lax.fori_loop(
        0, n_chunks, body,
        (init(mrows, MLA_KV_LORA), tuple(init(drows, LANES) for _ in range(DIFF_KV_HEADS))))

    nlat[0:dec_t, 0:MLA_KV_LORA] = rows_ref[:, 0:MLA_KV_LORA]
    nlat[0:dec_t, MLA_KV_LORA:MLA_LAT] = rows_ref[:, MLA_KV_LORA:MLA_LAT]
    for j in range(4):
        nkv[0:dec_t, j * LANES:(j + 1) * LANES] = kv_ref[pl.ds(j, dec_t, stride=4), :]
    new_k = nlat[...].astype(BF16)
    rr = lax.broadcasted_iota(jnp.int32, (MLA_HEADS, dec_t, LANES), 1).reshape(mrows, LANES)
    cc = lax.broadcasted_iota(jnp.int32, (mrows, LANES), 1)
    s = jnp.where(cc <= rr, _dot_nt(qm, new_k), MASK_VALUE)
    _, l_m, acc_m = _online_step(s, new_k[:, 0:MLA_KV_LORA], *st_m)
    o_m = acc_m * (1.0 / l_m)
    for hd in range(MLA_HEADS):
        om_ref[:, hd * LANES:(hd + 1) * LANES] = o_m[hd * dec_t:(hd + 1) * dec_t, :]

    lam = _diff_lambda(lq1_ref[...], lk1_ref[...], lq2_ref[...], lk2_ref[...], lam_init)
    for kvh in range(DIFF_KV_HEADS):
        kd = nkv[:, kvh * LANES:(kvh + 1) * LANES].astype(BF16)
        vd = nkv[:, (2 + kvh) * LANES:(3 + kvh) * LANES].astype(BF16)
        s = _dot_nt(qd[kvh], kd) + biasn_ref[kvh]
        _, l_d, acc_d = _online_step(s, vd, *st_d[kvh])
        o = acc_d * (1.0 / l_d)
        for g in range(DIFF_GROUP):
            o0 = o[(2 * g) * dec_t:(2 * g + 1) * dec_t, :]
            o1 = o[(2 * g + 1) * dec_t:(2 * g + 2) * dec_t, :]
            od_ref[:, (kvh * 2 + g) * LANES:(kvh * 2 + g + 1) * LANES] = _diff_combine(
                o0, o1, lam, subln_ref[...], lam_init)


def _decode(page_table, qm, qd, rows, kv, bias_d, bias_n, wl, cache_mla, cache_diff,
            lam_init, layer, dec_b, dec_t):
    n_pages = page_table.shape[1]
    assert n_pages % DEC_PAGES == 0 and dec_b * (n_pages // DEC_PAGES) >= DEC_SLOTS
    n_chunks = n_pages // DEC_PAGES
    n_pool = cache_diff.shape[1]
    cd = cache_diff.reshape(cache_diff.shape[0], n_pool, PAGE_SIZE * 4, LANES)
    cm = jnp.swapaxes(cache_mla, 2, 3)
    const = lambda b, pt: (0, 0)
    small = lambda a: pl.BlockSpec(a.shape, const)
    grid_spec = pltpu.PrefetchScalarGridSpec(
        num_scalar_prefetch=1,
        grid=(dec_b,),
        in_specs=[
            pl.BlockSpec((MLA_HEADS, dec_t, QPAD), lambda b, pt: (0, b, 0)),
            pl.BlockSpec((8, dec_t, LANES), lambda b, pt: (0, b, 0)),
            pl.BlockSpec((dec_t, MLA_LAT), lambda b, pt: (b, 0)),
            pl.BlockSpec((4 * dec_t, LANES), lambda b, pt: (b, 0)),
            pl.BlockSpec(bias_d.shape, lambda b, pt: (0, 0, 0, 0)),
            pl.BlockSpec(bias_n.shape, lambda b, pt: (0, 0, 0)),
            small(wl['lq1']), small(wl['lk1']), small(wl['lq2']), small(wl['lk2']), small(wl['subln_g']),
            pl.BlockSpec(memory_space=pl.ANY),
            pl.BlockSpec(memory_space=pl.ANY),
        ],
        out_specs=(
            pl.BlockSpec((dec_t, MLA_HEADS * MLA_KV_LORA), lambda b, pt: (b, 0)),
            pl.BlockSpec((dec_t, DIFF_HEADS * LANES), lambda b, pt: (b, 0)),
        ),
        scratch_shapes=[
            pltpu.VMEM((DEC_SLOTS, MLA_LAT, DEC_TK), F32),
            pltpu.VMEM((DEC_SLOTS, DEC_TK * 4, LANES), F32),
            pltpu.VMEM((LANES, QPAD), F32),
            pltpu.VMEM((LANES, 4 * LANES), F32),
            pltpu.SemaphoreType.DMA((2, DEC_SLOTS)),
        ],
    )
    return pl.pallas_call(
        functools.partial(_decode_kernel, lam_init=lam_init, layer=layer, n_chunks=n_chunks, dec_t=dec_t),
        out_shape=(
            jax.ShapeDtypeStruct((dec_b * dec_t, MLA_HEADS * MLA_KV_LORA), F32),
            jax.ShapeDtypeStruct((dec_b * dec_t, DIFF_HEADS * LANES), F32),
        ),
        grid_spec=grid_spec,
        compiler_params=_cparams("arbitrary"),
        name="paged_decode",
    )(page_table, qm, qd, rows, kv, bias_d, bias_n,
      wl['lq1'], wl['lk1'], wl['lq2'], wl['lk2'], wl['subln_g'], cm, cd)


def _merge_kernel(x_ref, om_ref, od_ref, pre_g_ref, wg_ref, wuv_ref, woa_ref, wob_ref, wout_ref,
                  post_g_ref, o_ref):
    x = x_ref[...]
    h = _rms(x, pre_g_ref[...]).astype(BF16)
    gates = _dot(h, wg_ref[...])
    v_a = _dot(om_ref[...].astype(BF16), wuv_ref[...])
    y_a = _dot(v_a.astype(BF16), woa_ref[...])
    y_b = _dot(od_ref[...].astype(BF16), wob_ref[...])
    z = jax.nn.sigmoid(gates[:, 0:D_MODEL]) * y_a + jax.nn.sigmoid(gates[:, D_MODEL:2 * D_MODEL]) * y_b
    y = _dot(z.astype(BF16), wout_ref[...])
    o_ref[...] = x + _rms(y, post_g_ref[...])


def _merge(x, o_m, o_d, wl, tm):
    n = x.shape[0]
    const = lambda i: (0, 0)
    w = lambda a: pl.BlockSpec(a.shape, const)
    row = lambda width: pl.BlockSpec((tm, width), lambda i: (i, 0))
    return pl.pallas_call(
        _merge_kernel,
        out_shape=jax.ShapeDtypeStruct((n, D_MODEL), F32),
        grid=(n // tm,),
        in_specs=[row(D_MODEL), row(o_m.shape[1]), row(o_d.shape[1]),
                  w(wl['mix_pre_g']), w(wl['w_gates']), w(wl['w_uv_bd']), w(wl['w_o_mla']),
                  w(wl['w_o_diff']), w(wl['w_out']), w(wl['mix_post_g'])],
        out_specs=row(D_MODEL),
        compiler_params=_cparams("parallel"),
        name="merge",
    )(x, o_m, o_d, wl['mix_pre_g'], wl['w_gates'], wl['w_uv_bd'], wl['w_o_mla'], wl['w_o_diff'],
      wl['w_out'], wl['mix_post_g'])


def _rope_tables(pos, n_rows):
    half = MLA_ROPE // 2
    inv_freq = ROPE_THETA ** (-jnp.arange(half, dtype=F32) / half)
    ang = pos.astype(F32)[:, None] * inv_freq[None, :]
    cos, sin = jnp.cos(ang), jnp.sin(ang)
    zeros = jnp.zeros((pos.shape[0], LANES - MLA_ROPE), F32)
    cos_t = jnp.concatenate([cos, cos, zeros], axis=1)
    sin_t = jnp.concatenate([-sin, sin, zeros], axis=1)
    reps = n_rows // pos.shape[0]
    return jnp.tile(cos_t, (reps, 1)), jnp.tile(sin_t, (reps, 1))


def _layer_weights(layer, p):
    g = lambda name: p[name][layer].reshape(1, -1)
    w_in = p['w_in'][layer]
    o = np.cumsum((MLA_Q_LORA, MLA_KV_LORA, MLA_ROPE, DIFF_HEADS * 2 * DIFF_DH,
                   DIFF_KV_HEADS * 2 * DIFF_DH, DIFF_KV_HEADS * 2 * DIFF_DH)).tolist()
    half = MLA_ROPE // 2
    kr = w_in[:, o[1]:o[2]]
    zpad = jnp.zeros((D_MODEL, LANES - MLA_ROPE), F32)
    kr_a = jnp.concatenate([kr, zpad], axis=1)
    kr_b = jnp.concatenate([kr[:, half:], kr[:, :half], zpad], axis=1)
    w_in_a = jnp.concatenate([w_in[:, :o[1]], kr_a, kr_b, w_in[:, o[2]:o[5]]], axis=1).astype(BF16)

    w_uq = p['w_uq'][layer].reshape(MLA_Q_LORA, MLA_HEADS, MLA_NOPE + MLA_ROPE)
    wq_nope = jnp.moveaxis(w_uq[:, :, :MLA_NOPE], 1, 0).astype(BF16)
    wk_t = jnp.transpose(p['w_uk'][layer], (1, 2, 0)).astype(BF16)
    x1 = w_uq[:, :, MLA_NOPE:MLA_NOPE + half]
    x2 = w_uq[:, :, MLA_NOPE + half:]
    hz = jnp.zeros((MLA_Q_LORA, MLA_HEADS, LANES - MLA_ROPE), F32)
    w_rope_a = jnp.concatenate([x1, x2, hz], axis=2).reshape(MLA_Q_LORA, MLA_HEADS * LANES).astype(BF16)
    w_rope_b = jnp.concatenate([x2, x1, hz], axis=2).reshape(MLA_Q_LORA, MLA_HEADS * LANES).astype(BF16)

    eye = jnp.eye(MLA_HEADS, dtype=F32)
    w_uv_bd = jnp.einsum('lhv,hg->hlgv', p['w_uv'][layer], eye).reshape(
        MLA_HEADS * MLA_KV_LORA, MLA_HEADS * MLA_V).astype(BF16)
    return {
        'ffn1_pre_g': g('ffn1_pre_g'), 'ffn1_w_gu': p['ffn1_w_gu'][layer].astype(BF16),
        'ffn1_w_down': p['ffn1_w_down'][layer].astype(BF16), 'ffn1_post_g': g('ffn1_post_g'),
        'ffn2_pre_g': g('ffn2_pre_g'), 'ffn2_w_gu': p['ffn2_w_gu'][layer].astype(BF16),
        'ffn2_w_down': p['ffn2_w_down'][layer].astype(BF16), 'ffn2_post_g': g('ffn2_post_g'),
        'mix_pre_g': g('mix_pre_g'), 'w_in_a': w_in_a, 'w_gates': w_in[:, o[5]:].astype(BF16),
        'q_norm_g': g('mla_q_norm_g'), 'kv_norm_g': g('mla_kv_norm_g'),
        'w_abs': _absorb(wq_nope, wk_t), 'w_rope_a': w_rope_a, 'w_rope_b': w_rope_b,
        'w_uv_bd': w_uv_bd, 'w_o_mla': p['w_o_mla'][layer].astype(BF16),
        'w_o_diff': p['w_o_diff'][layer].astype(BF16), 'w_out': p['w_out'][layer].astype(BF16),
        'mix_post_g': g('mix_post_g'),
        'lq1': g('diff_lq1'), 'lk1': g('diff_lk1'), 'lq2': g('diff_lq2'), 'lk2': g('diff_lk2'),
        'subln_g': g('diff_subln_g'),
    }


def _far_rows(rel_table, rows_per_map):
    far = (rel_table[REL_BUCKETS - 1].astype(F32) * LOG2E).reshape(DIFF_KV_HEADS, DIFF_GROUP, 1, 1)
    return jnp.broadcast_to(far, (DIFF_KV_HEADS, DIFF_GROUP, 2 * rows_per_map, LANES)).reshape(
        DIFF_KV_HEADS, 4 * rows_per_map, LANES)


def kernel(x_prompt, x_sample, cache_mla, cache_diff, page_table, ffn1_pre_g, ffn1_w_gu, ffn1_w_down, ffn1_post_g, mix_pre_g, w_in, mla_q_norm_g, w_uq, mla_kv_norm_g, w_uk, w_uv, diff_lq1, diff_lk1, diff_lq2, diff_lk2, diff_subln_g, rel_table, w_o_mla, w_o_diff, w_out, mix_post_g, ffn2_pre_g, ffn2_w_gu, ffn2_w_down, ffn2_post_g):
    params = dict(
        ffn1_pre_g=ffn1_pre_g, ffn1_w_gu=ffn1_w_gu, ffn1_w_down=ffn1_w_down, ffn1_post_g=ffn1_post_g,
        mix_pre_g=mix_pre_g, w_in=w_in, mla_q_norm_g=mla_q_norm_g, w_uq=w_uq,
        mla_kv_norm_g=mla_kv_norm_g, w_uk=w_uk, w_uv=w_uv, diff_lq1=diff_lq1, diff_lk1=diff_lk1,
        diff_lq2=diff_lq2, diff_lk2=diff_lk2, diff_subln_g=diff_subln_g, w_o_mla=w_o_mla,
        w_o_diff=w_o_diff, w_out=w_out, mix_post_g=mix_post_g, ffn2_pre_g=ffn2_pre_g,
        ffn2_w_gu=ffn2_w_gu, ffn2_w_down=ffn2_w_down, ffn2_post_g=ffn2_post_g)
    batch, seq, _ = x_prompt.shape
    dec_b, dec_t, _ = x_sample.shape
    depth = cache_mla.shape[0]
    past_len = page_table.shape[1] * PAGE_SIZE
    n_p, n_s = batch * seq, dec_b * dec_t
    tm_p, tm_s = 512, 512

    cos_p, sin_p = _rope_tables(jnp.arange(seq), seq)
    cos_s, sin_s = _rope_tables(past_len + jnp.arange(dec_t), tm_s)
    bias_p, bias_d, bias_n = _rel_bias_tiles(rel_table.astype(F32), past_len, dec_t)
    far_p = _far_rows(rel_table, PROMPT_TQ)

    y_p = x_prompt.reshape(n_p, D_MODEL)
    y_s = x_sample.reshape(n_s, D_MODEL)
    outs = ([], [], [], [])
    for layer in range(depth):
        wl = _layer_weights(layer, params)
        lam_init = 0.8 - 0.6 * math.exp(-0.3 * layer)

        y_p = _ffn(y_p, wl['ffn1_pre_g'], wl['ffn1_w_gu'], wl['ffn1_w_down'], wl['ffn1_post_g'], FFN_TM)
        q, lat_t, vx, rows_p, qd, kd_t, vdx, kv_p = _mix_in(y_p, wl, cos_p, sin_p, tm_p, BF16)
        o_m = _mla_prompt(q, lat_t, vx, batch, seq)
        o_d = _diff_prompt(qd, kd_t, vdx, bias_p, far_p, wl, lam_init, batch, seq)
        y_p = _merge(y_p, o_m, o_d, wl, tm_p)
        y_p = _ffn(y_p, wl['ffn2_pre_g'], wl['ffn2_w_gu'], wl['ffn2_w_down'], wl['ffn2_post_g'], FFN_TM)

        y_s = _ffn(y_s, wl['ffn1_pre_g'], wl['ffn1_w_gu'], wl['ffn1_w_down'], wl['ffn1_post_g'], FFN_TM)
        q, _, _, rows_s, qd, _, _, kv_s = _mix_in(y_s, wl, cos_s, sin_s, tm_s, F32)
        o_m, o_d = _decode(page_table, q, qd, rows_s, kv_s, bias_d, bias_n, wl,
                           cache_mla, cache_diff, lam_init, layer, dec_b, dec_t)
        y_s = _merge(y_s, o_m, o_d, wl, tm_s)
        y_s = _ffn(y_s, wl['ffn2_pre_g'], wl['ffn2_w_gu'], wl['ffn2_w_down'], wl['ffn2_post_g'], FFN_TM)

        outs[0].append(rows_p.reshape(batch, seq, MLA_LAT))
        outs[1].append(kv_p.reshape(batch, seq, 2, DIFF_KV_HEADS, 2 * DIFF_DH))
        outs[2].append(rows_s.reshape(dec_b, dec_t, MLA_LAT))
        outs[3].append(kv_s.reshape(dec_b, dec_t, 2, DIFF_KV_HEADS, 2 * DIFF_DH))
    return (y_p.reshape(batch, seq, D_MODEL), y_s.reshape(dec_b, dec_t, D_MODEL),
            jnp.stack(outs[0]), jnp.stack(outs[1]), jnp.stack(outs[2]), jnp.stack(outs[3]))
```
